```python
import math
import jax, jax.numpy as jnp
from jax import lax
import numpy as np

D_MODEL = 1024
BATCH = 8
SEQ = 2048
DEPTH = 4

CHUNK = 64
N_MIXERS = 3
N_LAYERS_A = len(range(0, DEPTH, N_MIXERS))
N_LAYERS_B = len(range(1, DEPTH, N_MIXERS))
N_LAYERS_C = len(range(2, DEPTH, N_MIXERS))
RMS_EPS = 1e-6
D_FF = 4 * D_MODEL
CONV_W = 4

GDN_DK = 128
GDN_DV = 128
GDN_HEADS = D_MODEL // GDN_DK
GDN_QK = GDN_HEADS * GDN_DK
GDN_V = GDN_HEADS * GDN_DV
GDN_CONV_CH = 2 * GDN_QK + GDN_V
GDN_IN = GDN_CONV_CH + GDN_V + 2 * GDN_HEADS

S5_GROUP = 16
S5_GROUPS = D_MODEL // S5_GROUP
S5_STATE = 64

M2_INNER = 2 * D_MODEL
M2_HEAD_DIM = 64
M2_HEADS = M2_INNER // M2_HEAD_DIM
M2_GROUPS = 8
M2_HPG = M2_HEADS // M2_GROUPS
M2_STATE = 128
M2_BC = M2_GROUPS * M2_STATE
M2_CONV_CH = M2_INNER + 2 * M2_BC
M2_IN = M2_INNER + M2_CONV_CH + M2_HEADS

kernel_name = "hybrid_gdn_s5_ssd_trunk"


def rmsnorm(x, g):
    xf = x.astype(jnp.float32)
    y = xf * lax.rsqrt(jnp.mean(xf * xf, axis=-1, keepdims=True) + RMS_EPS)
    return (y * g.astype(jnp.float32)).astype(x.dtype)


def causal_dwconv(x, w):
    return lax.conv_general_dilated(
        x, w[:, None, :], window_strides=(1,), padding=[(w.shape[0] - 1, 0)],
        dimension_numbers=("NWC", "WIO", "NWC"), feature_group_count=x.shape[-1])


def l2norm(t):
    return t * lax.rsqrt(jnp.sum(t * t, axis=-1, keepdims=True) + 1e-6)


def gated_deltanet(h, w_in, conv_w, a_log, dt_bias, o_norm_g, w_out):
    bsz, L, _ = h.shape
    nc = L // CHUNK
    f32 = jnp.float32
    proj = h @ w_in
    qkv, gate, a_raw, b_raw = jnp.split(
        proj, [GDN_CONV_CH, GDN_CONV_CH + GDN_V, GDN_CONV_CH + GDN_V + GDN_HEADS], axis=-1)
    qkv = jax.nn.silu(causal_dwconv(qkv, conv_w)).astype(f32)
    q, k, v = jnp.split(qkv, [GDN_QK, 2 * GDN_QK], axis=-1)
    q = l2norm(q.reshape(bsz, L, GDN_HEADS, GDN_DK)) * (GDN_DK ** -0.5)
    k = l2norm(k.reshape(bsz, L, GDN_HEADS, GDN_DK))
    v = v.reshape(bsz, L, GDN_HEADS, GDN_DV)
    g = -jnp.exp(a_log.astype(f32)) * jax.nn.softplus(a_raw.astype(f32) + dt_bias.astype(f32))
    beta = jax.nn.sigmoid(b_raw.astype(f32))

    def to_chunks(t):
        return t.reshape(bsz, nc, CHUNK, GDN_HEADS, -1).transpose(0, 3, 1, 2, 4)

    qc, kc, vc = to_chunks(q), to_chunks(k), to_chunks(v)
    gc = g.reshape(bsz, nc, CHUNK, GDN_HEADS).transpose(0, 3, 1, 2)
    bc = beta.reshape(bsz, nc, CHUNK, GDN_HEADS).transpose(0, 3, 1, 2)
    G = jnp.cumsum(gc, axis=-1)
    idx = jnp.arange(CHUNK)
    causal = idx[:, None] >= idx[None, :]
    strict = idx[:, None] > idx[None, :]
    decay = jnp.exp(jnp.where(causal, G[..., :, None] - G[..., None, :], -jnp.inf))
    kk = jnp.einsum('bhcid,bhcjd->bhcij', kc, kc)
    tri = jnp.where(strict, bc[..., :, None] * kk * decay, 0.0) + jnp.eye(CHUNK, dtype=f32)
    rhs = jnp.concatenate([vc * bc[..., None], kc * (bc * jnp.exp(G))[..., None]], axis=-1)
    sol = lax.linalg.triangular_solve(tri, rhs, left_side=True, lower=True, unit_diagonal=True)
    u, w = sol[..., :GDN_DV], sol[..., GDN_DV:]
    qk = jnp.einsum('bhcid,bhcjd->bhcij', qc, kc) * decay
    q_dec = qc * jnp.exp(G)[..., None]
    k_dec = kc * jnp.exp(G[..., -1:] - G)[..., None]
    chunk_decay = jnp.exp(G[..., -1])
    xs = tuple(jnp.moveaxis(t, 2, 0) for t in (u, w, qk, q_dec, k_dec, chunk_decay))

    def step(S, inp):
        u_c, w_c, qk_c, qd_c, kd_c, cd_c = inp
        v_new = u_c - jnp.einsum('bhid,bhde->bhie', w_c, S)
        o = jnp.einsum('bhid,bhde->bhie', qd_c, S) + jnp.einsum('bhij,bhje->bhie', qk_c, v_new)
        S = cd_c[..., None, None] * S + jnp.einsum('bhid,bhie->bhde', kd_c, v_new)
        return S, o

    S0 = jnp.zeros((bsz, GDN_HEADS, GDN_DK, GDN_DV), f32)
    _, o = lax.scan(step, S0, xs)
    o = o.transpose(1, 0, 3, 2, 4).reshape(bsz, L, GDN_HEADS, GDN_DV)
    o = rmsnorm(o, o_norm_g) * jax.nn.silu(gate.astype(f32).reshape(bsz, L, GDN_HEADS, GDN_DV))
    return o.reshape(bsz, L, GDN_V).astype(h.dtype) @ w_out


def s5_mixer(h, w_in, lam_re, lam_im, log_dt, b_re, b_im, c_re, c_im, d_skip, w_out):
    bsz, L, _ = h.shape
    f32 = jnp.float32
    u = (h @ w_in).astype(f32)
    ug = u.reshape(bsz, L, S5_GROUPS, S5_GROUP).transpose(1, 0, 2, 3)
    lam = lax.complex(lam_re.astype(f32), lam_im.astype(f32))
    dt = jnp.exp(log_dt.astype(f32))[:, None]
    lam_bar = jnp.exp(lam * dt)
    b_bar = ((lam_bar - 1.0) / lam)[..., None] * lax.complex(b_re.astype(f32), b_im.astype(f32))
    bu = jnp.einsum('gpk,lbgk->lbgp', b_bar, ug.astype(jnp.complex64))
    a = jnp.broadcast_to(lam_bar[None, None], (L, 1, S5_GROUPS, S5_STATE))

    def combine(e1, e2):
        a1, b1 = e1
        a2, b2 = e2
        return a1 * a2, a2 * b1 + b2

    _, states = lax.associative_scan(combine, (a, bu), axis=0)
    c = lax.complex(c_re.astype(f32), c_im.astype(f32))
    y = jnp.real(jnp.einsum('gkp,lbgp->lbgk', c, states)) \
        + d_skip.astype(f32).reshape(S5_GROUPS, S5_GROUP) * ug
    y = jax.nn.gelu(y.transpose(1, 0, 2, 3).reshape(bsz, L, D_MODEL)).astype(h.dtype)
    ag = y @ w_out
    val, gt = jnp.split(ag, 2, axis=-1)
    return val * jax.nn.sigmoid(gt)


def mamba2_mixer(h, w_in, conv_w, conv_b, dt_bias, a_log, d_skip, norm_g, w_out):
    bsz, L, _ = h.shape
    nc = L // CHUNK
    f32 = jnp.float32
    proj = h @ w_in
    z, xbc, dt_raw = jnp.split(proj, [M2_INNER, M2_INNER + M2_CONV_CH], axis=-1)
    xbc = jax.nn.silu(causal_dwconv(xbc, conv_w) + conv_b).astype(f32)
    xs, Bm, Cm = jnp.split(xbc, [M2_INNER, M2_INNER + M2_BC], axis=-1)
    x = xs.reshape(bsz, L, M2_HEADS, M2_HEAD_DIM)
    dt = jax.nn.softplus(dt_raw.astype(f32) + dt_bias.astype(f32))
    dA = dt * (-jnp.exp(a_log.astype(f32)))
    xdt = (x * dt[..., None]).reshape(bsz, nc, CHUNK, M2_GROUPS, M2_HPG, M2_HEAD_DIM)
    Bc = Bm.reshape(bsz, nc, CHUNK, M2_GROUPS, M2_STATE)
    Cc = Cm.reshape(bsz, nc, CHUNK, M2_GROUPS, M2_STATE)
    cum = jnp.cumsum(dA.reshape(bsz, nc, CHUNK, M2_GROUPS, M2_HPG), axis=2)
    idx = jnp.arange(CHUNK)
    causal = (idx[:, None] >= idx[None, :])[:, :, None, None]
    seg = cum[:, :, :, None] - cum[:, :, None, :]
    Lmat = jnp.exp(jnp.where(causal, seg, -jnp.inf))
    cb = jnp.einsum('bclgn,bcsgn->bclsg', Cc, Bc)
    y_diag = jnp.einsum('bclsgr,bcsgrp->bclgrp', cb[..., None] * Lmat, xdt)
    decay_states = jnp.exp(cum[:, :, -1:] - cum)
    states = jnp.einsum('bclgn,bclgrp->bcgrpn', Bc, xdt * decay_states[..., None])
    chunk_decay = jnp.exp(cum[:, :, -1])

    def step(S, inp):
        cd, st = inp
        return cd[..., None, None] * S + st, S

    S0 = jnp.zeros((bsz, M2_GROUPS, M2_HPG, M2_HEAD_DIM, M2_STATE), f32)
    _, S_prev = lax.scan(step, S0, (jnp.moveaxis(chunk_decay, 1, 0), jnp.moveaxis(states, 1, 0)))
    S_prev = jnp.moveaxis(S_prev, 0, 1)
    y_off = jnp.einsum('bclgn,bcgrpn->bclgrp', Cc, S_prev) * jnp.exp(cum)[..., None]
    y = (y_diag + y_off).reshape(bsz, L, M2_HEADS, M2_HEAD_DIM) + d_skip.astype(f32)[:, None] * x
    y = y.reshape(bsz, L, M2_INNER) * jax.nn.silu(z.astype(f32))
    y = rmsnorm(y.reshape(bsz, L, M2_GROUPS, M2_INNER // M2_GROUPS),
                norm_g.reshape(M2_GROUPS, M2_INNER // M2_GROUPS))
    return y.reshape(bsz, L, M2_INNER).astype(h.dtype) @ w_out


def sq_relu_mlp(h, w1, w2):
    return jnp.square(jax.nn.relu(h @ w1)) @ w2


def _inv_softplus_dt(key, shape):
    dt = jnp.exp(jax.random.uniform(key, shape, minval=math.log(1e-3), maxval=math.log(1e-1)))
    return dt + jnp.log(-jnp.expm1(-dt))


def setup_inputs(seed: int = 0) -> dict:
    key = jax.random.key(seed)
    ks = jax.random.split(key, 32)
    nrm = jax.random.normal
    f32 = jnp.float32
    nA, nB, nC = N_LAYERS_A, N_LAYERS_B, N_LAYERS_C
    return {
        "x": nrm(ks[0], (BATCH, SEQ, D_MODEL), f32),
        "norm_mix_g": 1.0 + 0.02 * nrm(ks[1], (DEPTH, D_MODEL), f32),
        "norm_mlp_g": 1.0 + 0.02 * nrm(ks[2], (DEPTH, D_MODEL), f32),
        "mlp_w1": nrm(ks[3], (DEPTH, D_MODEL, D_FF), f32) * D_MODEL ** -0.5,
        "mlp_w2": nrm(ks[4], (DEPTH, D_FF, D_MODEL), f32) * D_FF ** -0.5,
        "gdn_w_in": nrm(ks[5], (nA, D_MODEL, GDN_IN), f32) * D_MODEL ** -0.5,
        "gdn_conv_w": nrm(ks[6], (nA, CONV_W, GDN_CONV_CH), f32) * CONV_W ** -0.5,
        "gdn_a_log": jnp.log(jax.random.uniform(ks[7], (nA, GDN_HEADS), minval=1.0, maxval=16.0)),
        "gdn_dt_bias": _inv_softplus_dt(ks[8], (nA, GDN_HEADS)),
        "gdn_o_norm_g": 1.0 + 0.02 * nrm(ks[9], (nA, GDN_DV), f32),
        "gdn_w_out": nrm(ks[10], (nA, GDN_V, D_MODEL), f32) * GDN_V ** -0.5,
        "s5_w_in": nrm(ks[11], (nB, D_MODEL, D_MODEL), f32) * D_MODEL ** -0.5,
        "s5_lam_re": -0.5 + 0.01 * nrm(ks[12], (nB, S5_GROUPS, S5_STATE), f32),
        "s5_lam_im": math.pi * jnp.arange(S5_STATE, dtype=f32) + 0.01 * nrm(ks[13], (nB, S5_GROUPS, S5_STATE), f32),
        "s5_log_dt": jax.random.uniform(ks[14], (nB, S5_GROUPS), minval=math.log(1e-3), maxval=math.log(1e-1)),
        "s5_b_re": nrm(ks[15], (nB, S5_GROUPS, S5_STATE, S5_GROUP), f32) * (2 * S5_GROUP) ** -0.5,
        "s5_b_im": nrm(ks[16], (nB, S5_GROUPS, S5_STATE, S5_GROUP), f32) * (2 * S5_GROUP) ** -0.5,
        "s5_c_re": nrm(ks[17], (nB, S5_GROUPS, S5_GROUP, S5_STATE), f32) * (2 * S5_STATE) ** -0.5 * 4.0,
        "s5_c_im": nrm(ks[18], (nB, S5_GROUPS, S5_GROUP, S5_STATE), f32) * (2 * S5_STATE) ** -0.5 * 4.0,
        "s5_d": nrm(ks[19], (nB, D_MODEL), f32),
        "s5_w_out": nrm(ks[20], (nB, D_MODEL, 2 * D_MODEL), f32) * D_MODEL ** -0.5,
        "m2_w_in": nrm(ks[21], (nC, D_MODEL, M2_IN), f32) * D_MODEL ** -0.5,
        "m2_conv_w": nrm(ks[22], (nC, CONV_W, M2_CONV_CH), f32) * CONV_W ** -0.5,
        "m2_conv_b": 0.02 * nrm(ks[23], (nC, M2_CONV_CH), f32),
        "m2_dt_bias": _inv_softplus_dt(ks[24], (nC, M2_HEADS)),
        "m2_a_log": jnp.log(jax.random.uniform(ks[25], (nC, M2_HEADS), minval=1.0, maxval=16.0)),
        "m2_d": 1.0 + 0.02 * nrm(ks[26], (nC, M2_HEADS), f32),
        "m2_norm_g": 1.0 + 0.02 * nrm(ks[27], (nC, M2_INNER), f32),
        "m2_w_out": nrm(ks[28], (nC, M2_INNER, D_MODEL), f32) * M2_INNER ** -0.5,
        "final_norm_g": 1.0 + 0.02 * nrm(ks[29], (D_MODEL,), f32),
    }


def reference(x, norm_mix_g, norm_mlp_g, mlp_w1, mlp_w2,
              gdn_w_in, gdn_conv_w, gdn_a_log, gdn_dt_bias, gdn_o_norm_g, gdn_w_out,
              s5_w_in, s5_lam_re, s5_lam_im, s5_log_dt, s5_b_re, s5_b_im, s5_c_re, s5_c_im, s5_d, s5_w_out,
              m2_w_in, m2_conv_w, m2_conv_b, m2_dt_bias, m2_a_log, m2_d, m2_norm_g, m2_w_out,
              final_norm_g):
    h = x
    for i in range(DEPTH):
        kind, j = i % N_MIXERS, i // N_MIXERS
        hn = rmsnorm(h, norm_mix_g[i])
        if kind == 0:
            m = gated_deltanet(hn, gdn_w_in[j], gdn_conv_w[j], gdn_a_log[j], gdn_dt_bias[j],
                               gdn_o_norm_g[j], gdn_w_out[j])
        elif kind == 1:
            m = s5_mixer(hn, s5_w_in[j], s5_lam_re[j], s5_lam_im[j], s5_log_dt[j], s5_b_re[j],
                         s5_b_im[j], s5_c_re[j], s5_c_im[j], s5_d[j], s5_w_out[j])
        else:
            m = mamba2_mixer(hn, m2_w_in[j], m2_conv_w[j], m2_conv_b[j], m2_dt_bias[j], m2_a_log[j],
                             m2_d[j], m2_norm_g[j], m2_w_out[j])
        h = h + m.astype(h.dtype)
        h = h + sq_relu_mlp(rmsnorm(h, norm_mlp_g[i]), mlp_w1[i], mlp_w2[i]).astype(h.dtype)
    return rmsnorm(h, final_norm_g)
```

```python
import functools

import jax
import jax.numpy as jnp
from jax import lax
from jax.experimental import pallas as pl
from jax.experimental.pallas import tpu as pltpu

F32 = jnp.float32
BF16 = jnp.bfloat16
HIGHEST = lax.Precision.HIGHEST

RMS_EPS = 1e-6
L2_EPS = 1e-6
CHUNK = 64
CONV_W = 4
LANES = 128
SUBLANES = 8
VMEM_LIMIT = 56 * 1024 * 1024

GDN_HEADS = 8
GDN_DK = 128
GDN_DV = 128
S5_GROUP = 16
S5_STATE = 64
S5_SLAB = 128
M2_HEAD_DIM = 64
M2_STATE = 128
M2_GROUPS = 8


def _cparams(*sem):
    return pltpu.CompilerParams(dimension_semantics=sem, vmem_limit_bytes=VMEM_LIMIT)


def _rms(x, g):
    return x * lax.rsqrt(jnp.mean(x * x, axis=-1, keepdims=True) + RMS_EPS) * g


def _silu(x):
    return x * jax.nn.sigmoid(x)


def _softplus(x):
    return jnp.maximum(x, 0.0) + jnp.log1p(jnp.exp(-jnp.abs(x)))


def _dot(a, b):
    return jnp.dot(a, b, preferred_element_type=F32)


def _dot_f32(a, b):
    return jnp.dot(a, b, precision=HIGHEST, preferred_element_type=F32)


def _dot_nt(a, b):
    return lax.dot_general(a, b, (((1,), (1,)), ((), ())), preferred_element_type=F32)


def _dot_tn(a, b):
    return lax.dot_general(a, b, (((0,), (0,)), ((), ())), preferred_element_type=F32)


def _causal_masks(n):
    row = lax.broadcasted_iota(jnp.int32, (n, n), 0)
    col = lax.broadcasted_iota(jnp.int32, (n, n), 1)
    return row >= col, row > col


def _norm_mm_kernel(x_ref, g_ref, w_ref, o_ref, xn_ref):
    @pl.when(pl.program_id(1) == 0)
    def _():
        xn_ref[...] = _rms(x_ref[...], g_ref[...]).astype(BF16)

    o_ref[...] = _dot(xn_ref[...], w_ref[...]).astype(o_ref.dtype)


def _norm_mm_small_kernel(x_ref, g_ref, w_ref, ws_ref, o_ref, os_ref, xn_ref):
    @pl.when(pl.program_id(1) == 0)
    def _():
        xn = _rms(x_ref[...], g_ref[...]).astype(BF16)
        xn_ref[...] = xn
        os_ref[...] = _dot(xn, ws_ref[...])

    o_ref[...] = _dot(xn_ref[...], w_ref[...]).astype(o_ref.dtype)


def norm_matmul(h, g, w, w_small=None, *, tm=1024, tn=1024, time_major_batch=None):
    T, D = h.shape
    N = w.shape[1]
    tm = min(tm, T if time_major_batch is None else T // time_major_batch)
    nj = N // tn
    grid = (T // tm, nj)
    x_spec = pl.BlockSpec((tm, D), lambda i, j: (i, 0))
    g_spec = pl.BlockSpec((1, D), lambda i, j: (0, 0))
    w_spec = pl.BlockSpec((D, tn), lambda i, j: (0, j))
    if time_major_batch is None:
        out_shape = jax.ShapeDtypeStruct((T, N), F32)
        o_spec = pl.BlockSpec((tm, tn), lambda i, j: (i, j))
    else:
        B = time_major_batch
        nt = (T // B) // tm
        out_shape = jax.ShapeDtypeStruct((T // B, B * N), F32)
        o_spec = pl.BlockSpec((tm, tn), lambda i, j: (i % nt, (i // nt) * nj + j))
    scratch = [pltpu.VMEM((tm, D), BF16)]
    g2 = g.reshape(1, D)
    if w_small is None:
        return pl.pallas_call(
            _norm_mm_kernel, grid=grid, in_specs=[x_spec, g_spec, w_spec],
            out_specs=o_spec, out_shape=out_shape, scratch_shapes=scratch,
            compiler_params=_cparams("parallel", "arbitrary"), name="norm_matmul",
        )(h, g2, w)
    ns = w_small.shape[1]
    return pl.pallas_call(
        _norm_mm_small_kernel, grid=grid,
        in_specs=[x_spec, g_spec, w_spec, pl.BlockSpec((D, ns), lambda i, j: (0, 0))],
        out_specs=[o_spec, pl.BlockSpec((tm, ns), lambda i, j: (i, 0))],
        out_shape=[out_shape, jax.ShapeDtypeStruct((T, ns), F32)],
        scratch_shapes=scratch,
        compiler_params=_cparams("parallel", "arbitrary"), name="norm_matmul_small",
    )(h, g2, w, w_small)


def _mm_res_kernel(x_ref, w_ref, r_ref, o_ref):
    o_ref[...] = r_ref[...] + _dot(x_ref[...], w_ref[...])


def matmul_residual(x, w, res, *, tm=1024):
    T, K = x.shape
    N = w.shape[1]
    tm = min(tm, T)
    return pl.pallas_call(
        _mm_res_kernel, grid=(T // tm,),
        in_specs=[pl.BlockSpec((tm, K), lambda i: (i, 0)),
                  pl.BlockSpec((K, N), lambda i: (0, 0)),
                  pl.BlockSpec((tm, N), lambda i: (i, 0))],
        out_specs=pl.BlockSpec((tm, N), lambda i: (i, 0)),
        out_shape=jax.ShapeDtypeStruct((T, N), F32),
        compiler_params=_cparams("parallel"), name="matmul_residual",
    )(x, w, res)


def _glu_res_kernel(x_ref, w_ref, r_ref, o_ref):
    n = o_ref.shape[1]
    ag = _dot(x_ref[...], w_ref[...])
    o_ref[...] = r_ref[...] + ag[:, :n] * jax.nn.sigmoid(ag[:, n:])


def glu_residual_time_major(y_tm, w, res, *, batch, tm=1024):
    L = y_tm.shape[0]
    K = y_tm.shape[1] // batch
    N = w.shape[1] // 2
    tm = min(tm, L)
    nt = L // tm
    T = L * batch
    return pl.pallas_call(
        _glu_res_kernel, grid=(T // tm,),
        in_specs=[pl.BlockSpec((tm, K), lambda i: (i % nt, i // nt)),
                  pl.BlockSpec((K, 2 * N), lambda i: (0, 0)),
                  pl.BlockSpec((tm, N), lambda i: (i, 0))],
        out_specs=pl.BlockSpec((tm, N), lambda i: (i, 0)),
        out_shape=jax.ShapeDtypeStruct((T, N), F32),
        compiler_params=_cparams("parallel"), name="glu_residual",
    )(y_tm, w, res)


def _mlp_kernel(x_ref, g_ref, w1_ref, w2_ref, *rest, final):
    if final:
        gf_ref, o_ref, xn_ref, acc_ref = rest
    else:
        o_ref, xn_ref, acc_ref = rest
    j = pl.program_id(1)

    @pl.when(j == 0)
    def _():
        xn_ref[...] = _rms(x_ref[...], g_ref[...]).astype(BF16)
        acc_ref[...] = jnp.zeros_like(acc_ref)

    a = jnp.maximum(_dot(xn_ref[...], w1_ref[...]), 0.0)
    acc_ref[...] += _dot((a * a).astype(BF16), w2_ref[...])

    @pl.when(j == pl.num_programs(1) - 1)
    def _():
        y = x_ref[...] + acc_ref[...]
        o_ref[...] = _rms(y, gf_ref[...]) if final else y


def mlp_residual(h, g, w1, w2, g_final=None, *, tm=1024, tf=512):
    T, D = h.shape
    FF = w1.shape[1]
    tm = min(tm, T)
    final = g_final is not None
    in_specs = [pl.BlockSpec((tm, D), lambda i, j: (i, 0)),
                pl.BlockSpec((1, D), lambda i, j: (0, 0)),
                pl.BlockSpec((D, tf), lambda i, j: (0, j)),
                pl.BlockSpec((tf, D), lambda i, j: (j, 0))]
    args = [h, g.reshape(1, D), w1, w2]
    if final:
        in_specs.append(pl.BlockSpec((1, D), lambda i, j: (0, 0)))
        args.append(g_final.reshape(1, D))
    return pl.pallas_call(
        functools.partial(_mlp_kernel, final=final), grid=(T // tm, FF // tf),
        in_specs=in_specs,
        out_specs=pl.BlockSpec((tm, D), lambda i, j: (i, 0)),
        out_shape=jax.ShapeDtypeStruct((T, D), F32),
        scratch_shapes=[pltpu.VMEM((tm, D), BF16), pltpu.VMEM((tm, D), F32)],
        compiler_params=_cparams("parallel", "arbitrary"), name="mlp_residual",
    )(*args)


def _conv_window(cur, prev, xb_ref, first_chunk):
    xb_ref[0:SUBLANES, :] = jnp.where(first_chunk, 0.0, prev)
    xb_ref[SUBLANES:SUBLANES + CHUNK, :] = cur


def _causal_conv(xb_ref, cw_ref):
    base = SUBLANES - (CONV_W - 1)
    acc = xb_ref[base:base + CHUNK, :] * cw_ref[0:1, :]
    for w in range(1, CONV_W):
        acc = acc + xb_ref[base + w:base + w + CHUNK, :] * cw_ref[w:w + 1, :]
    return acc


def _chunk_cumsum(x, causal):
    return _dot_f32(causal.astype(F32), x)


def _prev_rows_spec(width, seq_len):
    blocks_per_seq = seq_len // SUBLANES
    blocks_per_chunk = CHUNK // SUBLANES
    return pl.BlockSpec(
        (SUBLANES, width),
        lambda b, c: (jnp.maximum(b * blocks_per_seq + c * blocks_per_chunk - 1, 0), 0))


def _unit_lower_inverse(a):
    n = a.shape[0]
    row = lax.broadcasted_iota(jnp.int32, (n, n), 0)
    col = lax.broadcasted_iota(jnp.int32, (n, n), 1)
    p = jnp.where(row == col, 1.0, 0.0) - a
    ak = a
    power = 2
    while power < n:
        ak = _dot_f32(ak, ak)
        p = p + _dot_f32(p, ak)
        power *= 2
    return p


def _gdn_kernel(cur_ref, prev_ref, ab_ref, cw_ref, alog_ref, dtb_ref, ong_ref,
                o_ref, xb_ref, s_ref):
    c = pl.program_id(1)
    nqk = GDN_HEADS * GDN_DK
    nconv = 2 * nqk + GDN_HEADS * GDN_DV

    @pl.when(c == 0)
    def _():
        s_ref[...] = jnp.zeros_like(s_ref)

    _conv_window(cur_ref[:, :nconv], prev_ref[:, :nconv], xb_ref, c == 0)
    qkv = _silu(_causal_conv(xb_ref, cw_ref))

    causal, strict = _causal_masks(CHUNK)
    ab = ab_ref[...]
    g_all = -jnp.exp(alog_ref[...]) * _softplus(ab + dtb_ref[...])
    beta_all = jax.nn.sigmoid(ab)
    G = _chunk_cumsum(g_all, causal)
    G_rows = jnp.concatenate([G, G], axis=0).T
    G_last = G[CHUNK - 1:CHUNK, :]
    eG = jnp.exp(G)
    eG_rev = jnp.exp(G_last - G)
    chunk_decay = jnp.exp(G_last)

    for h in range(GDN_HEADS):
        q = qkv[:, h * GDN_DK:(h + 1) * GDN_DK]
        k = qkv[:, nqk + h * GDN_DK:nqk + (h + 1) * GDN_DK]
        v = qkv[:, 2 * nqk + h * GDN_DV:2 * nqk + (h + 1) * GDN_DV]
        q = q * lax.rsqrt(jnp.sum(q * q, axis=-1, keepdims=True) + L2_EPS) * (GDN_DK ** -0.5)
        k = k * lax.rsqrt(jnp.sum(k * k, axis=-1, keepdims=True) + L2_EPS)
        beta = beta_all[:, GDN_HEADS + h:GDN_HEADS + h + 1]
        diff = G[:, h:h + 1] - G_rows[h:h + 1, 0:CHUNK]
        decay = jnp.where(causal, jnp.exp(jnp.where(causal, diff, 0.0)), 0.0)
        kb = k.astype(BF16)
        kk = _dot_nt(kb, kb)
        t_inv = _unit_lower_inverse(jnp.where(strict, beta * kk * decay, 0.0))
        u = _dot_f32(t_inv, v * beta)
        w = _dot_f32(t_inv, k * (beta * eG[:, h:h + 1]))
        qk = _dot_nt(q.astype(BF16), kb) * decay
        S = s_ref[h]
        Sb = S.astype(BF16)
        v_new = u - _dot(w.astype(BF16), Sb)
        vb = v_new.astype(BF16)
        o = _dot((q * eG[:, h:h + 1]).astype(BF16), Sb) + _dot(qk.astype(BF16), vb)
        k_dec = k * eG_rev[:, h:h + 1]
        s_ref[h] = chunk_decay[:, h:h + 1] * S + _dot_tn(k_dec.astype(BF16), vb)
        gate = cur_ref[:, nconv + h * GDN_DV:nconv + (h + 1) * GDN_DV]
        o_ref[:, h * GDN_DV:(h + 1) * GDN_DV] = (_rms(o, ong_ref[...]) * _silu(gate)).astype(o_ref.dtype)


def gdn_core(proj, ab, conv_w, a_log, dt_bias, o_norm_g, *, batch, seq_len):
    T, W = proj.shape
    nv = GDN_HEADS * GDN_DV
    nconv = conv_w.shape[1]
    nc = seq_len // CHUNK
    pad = lambda a: jnp.pad(a.reshape(1, -1), ((0, 0), (0, LANES - a.shape[-1])))
    return pl.pallas_call(
        _gdn_kernel, grid=(batch, nc),
        in_specs=[pl.BlockSpec((CHUNK, W), lambda b, c: (b * nc + c, 0)),
                  _prev_rows_spec(W, seq_len),
                  pl.BlockSpec((CHUNK, LANES), lambda b, c: (b * nc + c, 0)),
                  pl.BlockSpec((CONV_W, nconv), lambda b, c: (0, 0)),
                  pl.BlockSpec((1, LANES), lambda b, c: (0, 0)),
                  pl.BlockSpec((1, LANES), lambda b, c: (0, 0)),
                  pl.BlockSpec((1, GDN_DV), lambda b, c: (0, 0))],
        out_specs=pl.BlockSpec((CHUNK, nv), lambda b, c: (b * nc + c, 0)),
        out_shape=jax.ShapeDtypeStruct((T, nv), BF16),
        scratch_shapes=[pltpu.VMEM((SUBLANES + CHUNK, nconv), F32),
                        pltpu.VMEM((GDN_HEADS, GDN_DK, GDN_DV), F32)],
        compiler_params=_cparams("parallel", "arbitrary"), name="gdn_core",
    )(proj, proj, ab, conv_w, pad(a_log), pad(dt_bias), o_norm_g.reshape(1, GDN_DV))


def gdn_layer(h, g_mix, w_in, conv_w, a_log, dt_bias, o_norm_g, w_out, *, batch, seq_len):
    nmain = conv_w.shape[1] + GDN_HEADS * GDN_DV
    w_main = w_in[:, :nmain].astype(BF16)
    w_small = jnp.pad(w_in[:, nmain:], ((0, 0), (0, LANES - 2 * GDN_HEADS))).astype(BF16)
    proj, ab = norm_matmul(h, g_mix, w_main, w_small)
    o = gdn_core(proj, ab, conv_w, a_log, dt_bias, o_norm_g, batch=batch, seq_len=seq_len)
    return matmul_residual(o, w_out.astype(BF16), h)


def _m2_kernel(cur_ref, prev_ref, dt_ref, cw_ref, cb_ref, dtb_ref, alog_ref, dsk_ref,
               ng_ref, sel_ref, o_ref, xb_ref, y_ref, s_ref):
    c = pl.program_id(1)
    inner = dsk_ref.shape[1]
    nbc = M2_GROUPS * M2_STATE
    pair_w = 2 * M2_HEAD_DIM
    pairs_per_group = inner // M2_GROUPS // pair_w

    @pl.when(c == 0)
    def _():
        s_ref[...] = jnp.zeros_like(s_ref)

    _conv_window(cur_ref[:, inner:], prev_ref[:, inner:], xb_ref, c == 0)
    xbc = _silu(_causal_conv(xb_ref, cw_ref) + cb_ref[...])
    x = xbc[:, :inner]

    causal, _ = _causal_masks(CHUNK)
    dt = _softplus(dt_ref[...] + dtb_ref[...])
    cum = _chunk_cumsum(dt * (-jnp.exp(alog_ref[...])), causal)
    cum_rows = jnp.concatenate([cum, cum], axis=0).T
    spread = _dot_f32(jnp.concatenate([dt, cum], axis=0), sel_ref[...])
    dt_x = spread[:CHUNK]
    cum_x = spread[CHUNK:]
    cum_last = cum_x[CHUNK - 1:CHUNK, :]
    xdt = x * dt_x
    xd = xdt * jnp.exp(cum_last - cum_x)
    ecum = jnp.exp(cum_x)
    chunk_decay = jnp.exp(cum_last)

    lane = lax.broadcasted_iota(jnp.int32, (CHUNK, pair_w), 1)
    row = lax.broadcasted_iota(jnp.int32, (CHUNK, pair_w), 0)
    first = lane < M2_HEAD_DIM
    causal2 = row >= jnp.where(first, lane, lane - M2_HEAD_DIM)

    for g in range(M2_GROUPS):
        Bg = xbc[:, inner + g * M2_STATE:inner + (g + 1) * M2_STATE].astype(BF16)
        Cg = xbc[:, inner + nbc + g * M2_STATE:inner + nbc + (g + 1) * M2_STATE].astype(BF16)
        cb2 = _dot_nt(Cg, jnp.concatenate([Bg, Bg], axis=0))
        for m in range(pairs_per_group):
            p = g * pairs_per_group + m
            cols = slice(p * pair_w, (p + 1) * pair_w)
            h1 = 2 * p
            row_pair = jnp.where(first[0:1], cum_rows[h1:h1 + 1, :], cum_rows[h1 + 1:h1 + 2, :])
            diff = cum_x[:, cols] - row_pair
            lmat = jnp.where(causal2, jnp.exp(jnp.where(causal2, diff, 0.0)), 0.0)
            xp = xdt[:, cols]
            x_bd = jnp.concatenate([jnp.where(first, xp, 0.0), jnp.where(first, 0.0, xp)], axis=0)
            y_diag = _dot((cb2 * lmat).astype(BF16), x_bd.astype(BF16))
            S = s_ref[p]
            y_off = _dot(Cg, S.astype(BF16)) * ecum[:, cols]
            s_ref[p] = chunk_decay[:, cols] * S + _dot_tn(Bg, xd[:, cols].astype(BF16))
            y = y_diag + y_off + dsk_ref[:, cols] * x[:, cols]
            y_ref[:, cols] = y * _silu(cur_ref[:, cols])

    gw = inner // M2_GROUPS
    for g in range(M2_GROUPS):
        cols = slice(g * gw, (g + 1) * gw)
        o_ref[:, cols] = _rms(y_ref[:, cols], ng_ref[:, cols]).astype(o_ref.dtype)


def m2_core(proj, dt_raw, conv_w, conv_b, dt_bias, a_log, d_skip, norm_g, *, batch, seq_len):
    T, W = proj.shape
    nconv = conv_w.shape[1]
    inner = W - nconv
    heads = inner // M2_HEAD_DIM
    nc = seq_len // CHUNK
    pad = lambda a: jnp.pad(a.reshape(1, -1), ((0, 0), (0, LANES - a.shape[-1])))
    head_of_lane = jnp.arange(inner, dtype=jnp.int32) // M2_HEAD_DIM
    sel = (jnp.arange(LANES, dtype=jnp.int32)[:, None] == head_of_lane[None, :]).astype(F32)
    d_x = jnp.repeat(d_skip, M2_HEAD_DIM).reshape(1, inner)
    const = lambda shape: pl.BlockSpec(shape, lambda b, c: (0, 0))
    return pl.pallas_call(
        _m2_kernel, grid=(batch, nc),
        in_specs=[pl.BlockSpec((CHUNK, W), lambda b, c: (b * nc + c, 0)),
                  _prev_rows_spec(W, seq_len),
                  pl.BlockSpec((CHUNK, LANES), lambda b, c: (b * nc + c, 0)),
                  const((CONV_W, nconv)), const((1, nconv)), const((1, LANES)),
                  const((1, LANES)), const((1, inner)), const((1, inner)),
                  const((LANES, inner))],
        out_specs=pl.BlockSpec((CHUNK, inner), lambda b, c: (b * nc + c, 0)),
        out_shape=jax.ShapeDtypeStruct((T, inner), BF16),
        scratch_shapes=[pltpu.VMEM((SUBLANES + CHUNK, nconv), F32),
                        pltpu.VMEM((CHUNK, inner), F32),
                        pltpu.VMEM((heads // 2, M2_STATE, 2 * M2_HEAD_DIM), F32)],
        compiler_params=_cparams("parallel", "arbitrary"), name="m2_core",
    )(proj, proj, dt_raw, conv_w, conv_b.reshape(1, nconv), pad(dt_bias), pad(a_log),
      d_x, norm_g.reshape(1, inner), sel)


def m2_layer(h, g_mix, w_in, conv_w, conv_b, dt_bias, a_log, d_skip, norm_g, w_out,
             *, batch, seq_len):
    heads = dt_bias.shape[0]
    nmain = w_in.shape[1] - heads
    w_main = w_in[:, :nmain].astype(BF16)
    w_small = jnp.pad(w_in[:, nmain:], ((0, 0), (0, LANES - heads))).astype(BF16)
    proj, dt_raw = norm_matmul(h, g_mix, w_main, w_small)
    y = m2_core(proj, dt_raw, conv_w, conv_b, dt_bias, a_log, d_skip, norm_g,
                batch=batch, seq_len=seq_len)
    return matmul_residual(y, w_out.astype(BF16), h)


def _s5_prep_kernel(lr_ref, li_ref, ldt_ref, br_ref, bi_ref, lbr_ref, lbi_ref, bbr_ref, bbi_ref):
    lr, li = lr_ref[...], li_ref[...]
    dt = jnp.exp(ldt_ref[...])
    mag = jnp.exp(lr * dt)
    lbr = mag * jnp.cos(li * dt)
    lbi = mag * jnp.sin(li * dt)
    nr = lbr - 1.0
    den = lr * lr + li * li
    qr = (nr * lr + lbi * li) / den
    qi = (lbi * lr - nr * li) / den
    br, bi = br_ref[...], bi_ref[...]
    lbr_ref[...] = lbr
    lbi_ref[...] = lbi
    bbr_ref[...] = qr * br - qi * bi
    bbi_ref[...] = qr * bi + qi * br


def _s5_kernel(u_ref, b_ref, cre_ref, cim_ref, lam_ref, d_ref, y_ref, bu_ref, st_ref, *, steps):
    nslab = b_ref.shape[0]
    half = b_ref.shape[2] // 2

    @pl.when(pl.program_id(0) == 0)
    def _():
        st_ref[...] = jnp.zeros_like(st_ref)

    for s in range(nslab):
        us = u_ref[:, s * S5_SLAB:(s + 1) * S5_SLAB].astype(BF16)
        bu_ref[:, 2 * half * s:2 * half * (s + 1)] = _dot(us, b_ref[s])

    for s in range(nslab):
        re = slice(2 * half * s, 2 * half * s + half)
        im = slice(2 * half * s + half, 2 * half * (s + 1))
        lam_r = jnp.broadcast_to(lam_ref[:, re], (SUBLANES, half))
        lam_i = jnp.broadcast_to(lam_ref[:, im], (SUBLANES, half))

        def step(t, carry, re=re, im=im, lam_r=lam_r, lam_i=lam_i):
            xr, xi = carry
            rows = pl.ds(pl.multiple_of(t * SUBLANES, SUBLANES), SUBLANES)
            nr = lam_r * xr - lam_i * xi + bu_ref[rows, re]
            ni = lam_r * xi + lam_i * xr + bu_ref[rows, im]
            bu_ref[rows, re] = nr
            bu_ref[rows, im] = ni
            return nr, ni

        xr, xi = lax.fori_loop(0, steps, step, (st_ref[:, re], st_ref[:, im]))
        st_ref[:, re] = xr
        st_ref[:, im] = xi

    for s in range(nslab):
        re = slice(2 * half * s, 2 * half * s + half)
        im = slice(2 * half * s + half, 2 * half * (s + 1))
        cols = slice(s * S5_SLAB, (s + 1) * S5_SLAB)
        y = (_dot(bu_ref[:, re].astype(BF16), cre_ref[s])
             - _dot(bu_ref[:, im].astype(BF16), cim_ref[s])
             + d_ref[:, cols] * u_ref[:, cols])
        y_ref[:, cols] = jax.nn.gelu(y, approximate=True).astype(y_ref.dtype)


def s5_core(u_tm, lam_re, lam_im, log_dt, b_re, b_im, c_re, c_im, d_skip, *, batch, steps=32):
    assert batch == SUBLANES, "one time step's batch entries fill one sublane tile"
    rows, D = u_tm.shape
    steps = min(steps, rows // batch)
    G, P = lam_re.shape
    K = S5_GROUP
    nslab = D // S5_SLAB
    gps = S5_SLAB // K
    half = gps * P

    rep = lambda a: jnp.repeat(a, K, axis=-1)
    flat = jax.ShapeDtypeStruct((G, P * K), F32)
    lbr, lbi, bbr, bbi = pl.pallas_call(
        _s5_prep_kernel, out_shape=[flat] * 4, name="s5_prep",
    )(rep(lam_re), rep(lam_im), jnp.broadcast_to(log_dt[:, None], (G, P * K)),
      b_re.reshape(G, P * K), b_im.reshape(G, P * K))

    eye = jnp.eye(gps, dtype=F32)
    to_slab = lambda a: a.reshape(G, P, K).transpose(0, 2, 1).reshape(nslab, gps, K, P)
    b_blk = jnp.concatenate(
        [jnp.einsum('sjkp,jJ->sjkJp', to_slab(a), eye).reshape(nslab, S5_SLAB, half)
         for a in (bbr, bbi)], axis=-1).astype(BF16)
    c_blk = [jnp.einsum('sjkp,jJ->sJpjk', a.reshape(nslab, gps, K, P), eye)
             .reshape(nslab, half, S5_SLAB).astype(BF16) for a in (c_re, c_im)]
    lam = jnp.stack([lbr[:, ::K].reshape(nslab, half), lbi[:, ::K].reshape(nslab, half)],
                    axis=1).reshape(1, nslab * 2 * half)

    blk = steps * batch
    const3 = lambda shape: pl.BlockSpec(shape, lambda i: (0, 0, 0))
    return pl.pallas_call(
        functools.partial(_s5_kernel, steps=steps), grid=(rows // blk,),
        in_specs=[pl.BlockSpec((blk, D), lambda i: (i, 0)),
                  const3((nslab, S5_SLAB, 2 * half)),
                  const3((nslab, half, S5_SLAB)), const3((nslab, half, S5_SLAB)),
                  pl.BlockSpec((1, nslab * 2 * half), lambda i: (0, 0)),
                  pl.BlockSpec((1, D), lambda i: (0, 0))],
        out_specs=pl.BlockSpec((blk, D), lambda i: (i, 0)),
        out_shape=jax.ShapeDtypeStruct((rows, D), BF16),
        scratch_shapes=[pltpu.VMEM((blk, nslab * 2 * half), F32),
                        pltpu.VMEM((SUBLANES, nslab * 2 * half), F32)],
        compiler_params=_cparams("arbitrary"), name="s5_core",
    )(u_tm, b_blk, c_blk[0], c_blk[1], lam, d_skip.reshape(1, D))


def s5_layer(h, g_mix, w_in, lam_re, lam_im, log_dt, b_re, b_im, c_re, c_im, d_skip, w_out,
             *, batch, seq_len):
    D = h.shape[1]
    u = norm_matmul(h, g_mix, w_in.astype(BF16), time_major_batch=batch)
    y = s5_core(u.reshape(seq_len * batch, D), lam_re, lam_im, log_dt, b_re, b_im, c_re, c_im,
                d_skip, batch=batch)
    return glu_residual_time_major(y.reshape(seq_len, batch * D), w_out.astype(BF16), h,
                                   batch=batch)


def kernel(x, norm_mix_g, norm_mlp_g, mlp_w1, mlp_w2, gdn_w_in, gdn_conv_w, gdn_a_log, gdn_dt_bias, gdn_o_norm_g, gdn_w_out, s5_w_in, s5_lam_re, s5_lam_im, s5_log_dt, s5_b_re, s5_b_im, s5_c_re, s5_c_im, s5_d, s5_w_out, m2_w_in, m2_conv_w, m2_conv_b, m2_dt_bias, m2_a_log, m2_d, m2_norm_g, m2_w_out, final_norm_g):
    batch, seq_len, d_model = x.shape
    depth = norm_mix_g.shape[0]
    h = x.reshape(batch * seq_len, d_model)
    for i in range(depth):
        kind, j = i % 3, i // 3
        if kind == 0:
            h = gdn_layer(h, norm_mix_g[i], gdn_w_in[j], gdn_conv_w[j], gdn_a_log[j],
                          gdn_dt_bias[j], gdn_o_norm_g[j], gdn_w_out[j],
                          batch=batch, seq_len=seq_len)
        elif kind == 1:
            h = s5_layer(h, norm_mix_g[i], s5_w_in[j], s5_lam_re[j], s5_lam_im[j], s5_log_dt[j],
                         s5_b_re[j], s5_b_im[j], s5_c_re[j], s5_c_im[j], s5_d[j], s5_w_out[j],
                         batch=batch, seq_len=seq_len)
        else:
            h = m2_layer(h, norm_mix_g[i], m2_w_in[j], m2_conv_w[j], m2_conv_b[j], m2_dt_bias[j],
                         m2_a_log[j], m2_d[j], m2_norm_g[j], m2_w_out[j],
                         batch=batch, seq_len=seq_len)
        h = mlp_residual(h, norm_mlp_g[i], mlp_w1[i].astype(BF16), mlp_w2[i].astype(BF16),
                         final_norm_g if i == depth - 1 else None)
    return h.reshape(batch, seq_len, d_model)
```

```python
import functools

import jax
import jax.numpy as jnp
from jax import lax
from jax.experimental import pallas as pl
from jax.experimental.pallas import tpu as pltpu

F32 = jnp.float32
BF16 = jnp.bfloat16

RMS_EPS = 1e-6
L2_EPS = 1e-6
CHUNK = 64
CONV_W = 4
LANES = 128
SUBLANES = 8
VMEM_LIMIT = 56 * 1024 * 1024

GDN_HEADS = 8
GDN_DK = 128
GDN_DV = 128
S5_GROUP = 16
S5_STATE = 64
S5_SLAB = 128
M2_HEAD_DIM = 64
M2_STATE = 128
M2_GROUPS = 8


def _cparams(*sem):
    return pltpu.CompilerParams(dimension_semantics=sem, vmem_limit_bytes=VMEM_LIMIT)


def _rms(x, g):
    return x * lax.rsqrt(jnp.mean(x * x, axis=-1, keepdims=True) + RMS_EPS) * g


def _silu(x):
    return x * jax.nn.sigmoid(x)


def _softplus(x):
    return jnp.maximum(x, 0.0) + jnp.log1p(jnp.exp(-jnp.abs(x)))


def _dot(a, b):
    return jnp.dot(a, b, preferred_element_type=F32)


def _split2(x):
    hi = x.astype(BF16)
    return hi, (x - hi.astype(F32)).astype(BF16)


def _split3(x):
    hi = x.astype(BF16)
    r = x - hi.astype(F32)
    mid = r.astype(BF16)
    return hi, mid, (r - mid.astype(F32)).astype(BF16)


def _dot_split(a, b):
    return _dot(a[0], b[0]) + (_dot(a[0], b[1]) + _dot(a[1], b[0]))


def _dot_nt(a, b):
    return lax.dot_general(a, b, (((1,), (1,)), ((), ())), preferred_element_type=F32)


def _dot_tn(a, b):
    return lax.dot_general(a, b, (((0,), (0,)), ((), ())), preferred_element_type=F32)


def _causal_masks(n):
    row = lax.broadcasted_iota(jnp.int32, (n, n), 0)
    col = lax.broadcasted_iota(jnp.int32, (n, n), 1)
    return row >= col, row > col


def _norm_mm_kernel(x_ref, g_ref, w_ref, o_ref, xn_ref):
    @pl.when(pl.program_id(1) == 0)
    def _():
        xn_ref[...] = _rms(x_ref[...], g_ref[...]).astype(BF16)

    o_ref[...] = _dot(xn_ref[...], w_ref[...]).astype(o_ref.dtype)


def _norm_mm_small_kernel(x_ref, g_ref, w_ref, ws_ref, o_ref, os_ref, xn_ref):
    @pl.when(pl.program_id(1) == 0)
    def _():
        xn = _rms(x_ref[...], g_ref[...]).astype(BF16)
        xn_ref[...] = xn
        os_ref[...] = _dot(xn, ws_ref[...])

    o_ref[...] = _dot(xn_ref[...], w_ref[...]).astype(o_ref.dtype)


def norm_matmul(h, g, w, w_small=None, *, tm=1024, tn=1024, time_major_batch=None):
    T, D = h.shape
    N = w.shape[1]
    tm = min(tm, T if time_major_batch is None else T // time_major_batch)
    nj = N // tn
    grid = (T // tm, nj)
    x_spec = pl.BlockSpec((tm, D), lambda i, j: (i, 0))
    g_spec = pl.BlockSpec((1, D), lambda i, j: (0, 0))
    w_spec = pl.BlockSpec((D, tn), lambda i, j: (0, j))
    if time_major_batch is None:
        out_shape = jax.ShapeDtypeStruct((T, N), F32)
        o_spec = pl.BlockSpec((tm, tn), lambda i, j: (i, j))
    else:
        B = time_major_batch
        nt = (T // B) // tm
        out_shape = jax.ShapeDtypeStruct((T // B, B * N), F32)
        o_spec = pl.BlockSpec((tm, tn), lambda i, j: (i % nt, (i // nt) * nj + j))
    scratch = [pltpu.VMEM((tm, D), BF16)]
    g2 = g.reshape(1, D)
    if w_small is None:
        return pl.pallas_call(
            _norm_mm_kernel, grid=grid, in_specs=[x_spec, g_spec, w_spec],
            out_specs=o_spec, out_shape=out_shape, scratch_shapes=scratch,
            compiler_params=_cparams("parallel", "arbitrary"), name="norm_matmul",
        )(h, g2, w)
    ns = w_small.shape[1]
    return pl.pallas_call(
        _norm_mm_small_kernel, grid=grid,
        in_specs=[x_spec, g_spec, w_spec, pl.BlockSpec((D, ns), lambda i, j: (0, 0))],
        out_specs=[o_spec, pl.BlockSpec((tm, ns), lambda i, j: (i, 0))],
        out_shape=[out_shape, jax.ShapeDtypeStruct((T, ns), F32)],
        scratch_shapes=scratch,
        compiler_params=_cparams("parallel", "arbitrary"), name="norm_matmul_small",
    )(h, g2, w, w_small)


def _mm_res_kernel(x_ref, w_ref, r_ref, o_ref):
    o_ref[...] = r_ref[...] + _dot(x_ref[...], w_ref[...])


def matmul_residual(x, w, res, *, tm=1024):
    T, K = x.shape
    N = w.shape[1]
    tm = min(tm, T)
    return pl.pallas_call(
        _mm_res_kernel, grid=(T // tm,),
        in_specs=[pl.BlockSpec((tm, K), lambda i: (i, 0)),
                  pl.BlockSpec((K, N), lambda i: (0, 0)),
                  pl.BlockSpec((tm, N), lambda i: (i, 0))],
        out_specs=pl.BlockSpec((tm, N), lambda i: (i, 0)),
        out_shape=jax.ShapeDtypeStruct((T, N), F32),
        compiler_params=_cparams("parallel"), name="matmul_residual",
    )(x, w, res)


def _glu_res_kernel(x_ref, w_ref, r_ref, o_ref):
    n = o_ref.shape[1]
    ag = _dot(x_ref[...], w_ref[...])
    o_ref[...] = r_ref[...] + ag[:, :n] * jax.nn.sigmoid(ag[:, n:])


def glu_residual_time_major(y_tm, w, res, *, batch, tm=1024):
    L = y_tm.shape[0]
    K = y_tm.shape[1] // batch
    N = w.shape[1] // 2
    tm = min(tm, L)
    nt = L // tm
    T = L * batch
    return pl.pallas_call(
        _glu_res_kernel, grid=(T // tm,),
        in_specs=[pl.BlockSpec((tm, K), lambda i: (i % nt, i // nt)),
                  pl.BlockSpec((K, 2 * N), lambda i: (0, 0)),
                  pl.BlockSpec((tm, N), lambda i: (i, 0))],
        out_specs=pl.BlockSpec((tm, N), lambda i: (i, 0)),
        out_shape=jax.ShapeDtypeStruct((T, N), F32),
        compiler_params=_cparams("parallel"), name="glu_residual",
    )(y_tm, w, res)


def _mlp_kernel(x_ref, g_ref, w1_ref, w2_ref, *rest, final):
    if final:
        gf_ref, o_ref, xn_ref, acc_ref = rest
    else:
        o_ref, xn_ref, acc_ref = rest
    j = pl.program_id(1)

    @pl.when(j == 0)
    def _():
        xn_ref[...] = _rms(x_ref[...], g_ref[...]).astype(BF16)
        acc_ref[...] = jnp.zeros_like(acc_ref)

    a = jnp.maximum(_dot(xn_ref[...], w1_ref[...]), 0.0)
    acc_ref[...] += _dot((a * a).astype(BF16), w2_ref[...])

    @pl.when(j == pl.num_programs(1) - 1)
    def _():
        y = x_ref[...] + acc_ref[...]
        o_ref[...] = _rms(y, gf_ref[...]) if final else y


def mlp_residual(h, g, w1, w2, g_final=None, *, tm=1024, tf=512):
    T, D = h.shape
    FF = w1.shape[1]
    tm = min(tm, T)
    final = g_final is not None
    in_specs = [pl.BlockSpec((tm, D), lambda i, j: (i, 0)),
                pl.BlockSpec((1, D), lambda i, j: (0, 0)),
                pl.BlockSpec((D, tf), lambda i, j: (0, j)),
                pl.BlockSpec((tf, D), lambda i, j: (j, 0))]
    args = [h, g.reshape(1, D), w1, w2]
    if final:
        in_specs.append(pl.BlockSpec((1, D), lambda i, j: (0, 0)))
        args.append(g_final.reshape(1, D))
    return pl.pallas_call(
        functools.partial(_mlp_kernel, final=final), grid=(T // tm, FF // tf),
        in_specs=in_specs,
        out_specs=pl.BlockSpec((tm, D), lambda i, j: (i, 0)),
        out_shape=jax.ShapeDtypeStruct((T, D), F32),
        scratch_shapes=[pltpu.VMEM((tm, D), BF16), pltpu.VMEM((tm, D), F32)],
        compiler_params=_cparams("parallel", "arbitrary"), name="mlp_residual",
    )(*args)


def _conv_window(cur, prev, xb_ref, first_chunk):
    xb_ref[0:SUBLANES, :] = jnp.where(first_chunk, 0.0, prev)
    xb_ref[SUBLANES:SUBLANES + CHUNK, :] = cur


def _causal_conv(xb_ref, cw_ref):
    base = SUBLANES - (CONV_W - 1)
    acc = xb_ref[base:base + CHUNK, :] * cw_ref[0:1, :]
    for w in range(1, CONV_W):
        acc = acc + xb_ref[base + w:base + w + CHUNK, :] * cw_ref[w:w + 1, :]
    return acc


def _chunk_cumsum(x, causal):
    n = x.shape[1]
    parts = _dot(jnp.where(causal, 1.0, 0.0).astype(BF16), jnp.concatenate(_split3(x), axis=1))
    return parts[:, :n] + (parts[:, n:2 * n] + parts[:, 2 * n:])


def _prev_rows_spec(width, seq_len):
    blocks_per_seq = seq_len // SUBLANES
    blocks_per_chunk = CHUNK // SUBLANES
    return pl.BlockSpec(
        (SUBLANES, width),
        lambda b, c: (jnp.maximum(b * blocks_per_seq + c * blocks_per_chunk - 1, 0), 0))


def _unit_lower_inverses(mats):
    n = mats[0].shape[0]
    row = lax.broadcasted_iota(jnp.int32, (n, n), 0)
    col = lax.broadcasted_iota(jnp.int32, (n, n), 1)
    eye = jnp.where(row == col, 1.0, 0.0)
    ps = [eye - a for a in mats]
    aks = [_split2(a) for a in mats]
    power = 2
    while power < n:
        aks = [_split2(_dot_split(ak, ak)) for ak in aks]
        ps = [p + _dot_split(_split2(p), ak) for p, ak in zip(ps, aks)]
        power *= 2
    return ps


def _gdn_kernel(cur_ref, prev_ref, ab_ref, cw_ref, alog_ref, dtb_ref, ong_ref,
                o_ref, xb_ref, s_ref):
    c = pl.program_id(1)
    nqk = GDN_HEADS * GDN_DK
    nconv = 2 * nqk + GDN_HEADS * GDN_DV
    heads = range(GDN_HEADS)

    @pl.when(c == 0)
    def _():
        s_ref[...] = jnp.zeros_like(s_ref)

    _conv_window(cur_ref[:, :nconv], prev_ref[:, :nconv], xb_ref, c == 0)
    qkv = _silu(_causal_conv(xb_ref, cw_ref))

    causal, strict = _causal_masks(CHUNK)
    ab = ab_ref[...]
    g_all = -jnp.exp(alog_ref[...]) * _softplus(ab + dtb_ref[...])
    beta_all = jax.nn.sigmoid(ab)
    G = _chunk_cumsum(g_all, causal)
    G_rows = jnp.concatenate([G, G], axis=0).T
    G_last = G[CHUNK - 1:CHUNK, :]
    eG = jnp.exp(G)
    eG_rev = jnp.exp(G_last - G)
    chunk_decay = jnp.exp(G_last)

    def l2n(t):
        return t * lax.rsqrt(jnp.sum(t * t, axis=-1, keepdims=True) + L2_EPS)

    q = [l2n(qkv[:, h * GDN_DK:(h + 1) * GDN_DK]) * (GDN_DK ** -0.5) for h in heads]
    k = [l2n(qkv[:, nqk + h * GDN_DK:nqk + (h + 1) * GDN_DK]) for h in heads]
    v = [qkv[:, 2 * nqk + h * GDN_DV:2 * nqk + (h + 1) * GDN_DV] for h in heads]
    beta = [beta_all[:, GDN_HEADS + h:GDN_HEADS + h + 1] for h in heads]
    eg = [eG[:, h:h + 1] for h in heads]
    decay = []
    for h in heads:
        diff = G[:, h:h + 1] - G_rows[h:h + 1, 0:CHUNK]
        decay.append(jnp.where(causal, jnp.exp(jnp.where(causal, diff, 0.0)), 0.0))
    kb = [t.astype(BF16) for t in k]
    qkk = [_dot_nt(jnp.concatenate([q[h].astype(BF16), kb[h]], axis=0), kb[h]) for h in heads]
    t_inv = _unit_lower_inverses(
        [jnp.where(strict, beta[h] * qkk[h][CHUNK:] * decay[h], 0.0) for h in heads])
    uw = [_dot(t_inv[h].astype(BF16),
               jnp.concatenate([v[h] * beta[h], k[h] * (beta[h] * eg[h])], axis=1).astype(BF16))
          for h in heads]
    S = [s_ref[h] for h in heads]
    Sb = [t.astype(BF16) for t in S]
    ws = [_dot(jnp.concatenate([uw[h][:, GDN_DV:], q[h] * eg[h]], axis=0).astype(BF16), Sb[h])
          for h in heads]
    vb = [(uw[h][:, :GDN_DV] - ws[h][:CHUNK]).astype(BF16) for h in heads]
    o = [ws[h][CHUNK:] + _dot((qkk[h][:CHUNK] * decay[h]).astype(BF16), vb[h]) for h in heads]
    s_new = [chunk_decay[:, h:h + 1] * S[h]
             + _dot_tn((k[h] * eG_rev[:, h:h + 1]).astype(BF16), vb[h]) for h in heads]
    for h in heads:
        s_ref[h] = s_new[h]
    for h in heads:
        gate = cur_ref[:, nconv + h * GDN_DV:nconv + (h + 1) * GDN_DV]
        o_ref[:, h * GDN_DV:(h + 1) * GDN_DV] = (_rms(o[h], ong_ref[...]) * _silu(gate)).astype(o_ref.dtype)


def gdn_core(proj, ab, conv_w, a_log, dt_bias, o_norm_g, *, batch, seq_len):
    T, W = proj.shape
    nv = GDN_HEADS * GDN_DV
    nconv = conv_w.shape[1]
    nc = seq_len // CHUNK
    pad = lambda a: jnp.pad(a.reshape(1, -1), ((0, 0), (0, LANES - a.shape[-1])))
    return pl.pallas_call(
        _gdn_kernel, grid=(batch, nc),
        in_specs=[pl.BlockSpec((CHUNK, W), lambda b, c: (b * nc + c, 0)),
                  _prev_rows_spec(W, seq_len),
                  pl.BlockSpec((CHUNK, LANES), lambda b, c: (b * nc + c, 0)),
                  pl.BlockSpec((CONV_W, nconv), lambda b, c: (0, 0)),
                  pl.BlockSpec((1, LANES), lambda b, c: (0, 0)),
                  pl.BlockSpec((1, LANES), lambda b, c: (0, 0)),
                  pl.BlockSpec((1, GDN_DV), lambda b, c: (0, 0))],
        out_specs=pl.BlockSpec((CHUNK, nv), lambda b, c: (b * nc + c, 0)),
        out_shape=jax.ShapeDtypeStruct((T, nv), BF16),
        scratch_shapes=[pltpu.VMEM((SUBLANES + CHUNK, nconv), F32),
                        pltpu.VMEM((GDN_HEADS, GDN_DK, GDN_DV), F32)],
        compiler_params=_cparams("parallel", "arbitrary"), name="gdn_core",
    )(proj, proj, ab, conv_w, pad(a_log), pad(dt_bias), o_norm_g.reshape(1, GDN_DV))


def gdn_layer(h, g_mix, w_in, conv_w, a_log, dt_bias, o_norm_g, w_out, *, batch, seq_len):
    nmain = conv_w.shape[1] + GDN_HEADS * GDN_DV
    w_main = w_in[:, :nmain].astype(BF16)
    w_small = jnp.pad(w_in[:, nmain:], ((0, 0), (0, LANES - 2 * GDN_HEADS))).astype(BF16)
    proj, ab = norm_matmul(h, g_mix, w_main, w_small)
    o = gdn_core(proj, ab, conv_w, a_log, dt_bias, o_norm_g, batch=batch, seq_len=seq_len)
    return matmul_residual(o, w_out.astype(BF16), h)


def _m2_kernel(cur_ref, prev_ref, dt_ref, cw_ref, cb_ref, dtb_ref, alog_ref, dsk_ref,
               ng_ref, sel_ref, o_ref, xb_ref, s_ref):
    c = pl.program_id(1)
    inner = dsk_ref.shape[1]
    nbc = M2_GROUPS * M2_STATE
    pair_w = 2 * M2_HEAD_DIM
    pairs_per_group = inner // M2_GROUPS // pair_w

    @pl.when(c == 0)
    def _():
        s_ref[...] = jnp.zeros_like(s_ref)

    _conv_window(cur_ref[:, inner:], prev_ref[:, inner:], xb_ref, c == 0)
    xbc = _silu(_causal_conv(xb_ref, cw_ref) + cb_ref[...])
    x = xbc[:, :inner]

    causal, _ = _causal_masks(CHUNK)
    dt = _softplus(dt_ref[...] + dtb_ref[...])
    cum = _chunk_cumsum(dt * (-jnp.exp(alog_ref[...])), causal)
    cum_rows = jnp.concatenate([cum, cum], axis=0).T
    parts = _dot(jnp.concatenate(_split3(jnp.concatenate([dt, cum], axis=0)), axis=0), sel_ref[...])
    spread = parts[:2 * CHUNK] + (parts[2 * CHUNK:4 * CHUNK] + parts[4 * CHUNK:])
    dt_x = spread[:CHUNK]
    cum_x = spread[CHUNK:]
    cum_last = cum_x[CHUNK - 1:CHUNK, :]
    xdt = x * dt_x
    xd = xdt * jnp.exp(cum_last - cum_x)
    ecum = jnp.exp(cum_x)
    chunk_decay = jnp.exp(cum_last)

    lane = lax.broadcasted_iota(jnp.int32, (CHUNK, pair_w), 1)
    row = lax.broadcasted_iota(jnp.int32, (CHUNK, pair_w), 0)
    first = lane < M2_HEAD_DIM
    causal2 = row >= jnp.where(first, lane, lane - M2_HEAD_DIM)

    groups = range(M2_GROUPS)
    gw = inner // M2_GROUPS
    gcols = [slice(g * gw, (g + 1) * gw) for g in groups]
    Bg = [xbc[:, inner + g * M2_STATE:inner + (g + 1) * M2_STATE].astype(BF16) for g in groups]
    Cg = [xbc[:, inner + nbc + g * M2_STATE:inner + nbc + (g + 1) * M2_STATE].astype(BF16)
          for g in groups]
    cb2 = [_dot_nt(Cg[g], jnp.concatenate([Bg[g], Bg[g]], axis=0)) for g in groups]
    S = [s_ref[g] for g in groups]
    y_off = [_dot(Cg[g], S[g].astype(BF16)) * ecum[:, gcols[g]] for g in groups]
    s_new = [chunk_decay[:, gcols[g]] * S[g] + _dot_tn(Bg[g], xd[:, gcols[g]].astype(BF16))
             for g in groups]
    for g in groups:
        s_ref[g] = s_new[g]
    y_diag = []
    for p in range(M2_GROUPS * pairs_per_group):
        cols = slice(p * pair_w, (p + 1) * pair_w)
        h1 = 2 * p
        row_pair = jnp.where(first[0:1], cum_rows[h1:h1 + 1, :], cum_rows[h1 + 1:h1 + 2, :])
        diff = cum_x[:, cols] - row_pair
        lmat = jnp.where(causal2, jnp.exp(jnp.where(causal2, diff, 0.0)), 0.0)
        xp = xdt[:, cols]
        x_bd = jnp.concatenate([jnp.where(first, xp, 0.0), jnp.where(first, 0.0, xp)], axis=0)
        y_diag.append(_dot((cb2[p // pairs_per_group] * lmat).astype(BF16), x_bd.astype(BF16)))
    for g in groups:
        cols = gcols[g]
        yd = jnp.concatenate(y_diag[g * pairs_per_group:(g + 1) * pairs_per_group], axis=1)
        y = (yd + y_off[g] + dsk_ref[:, cols] * x[:, cols]) * _silu(cur_ref[:, cols])
        o_ref[:, cols] = _rms(y, ng_ref[:, cols]).astype(o_ref.dtype)


def m2_core(proj, dt_raw, conv_w, conv_b, dt_bias, a_log, d_skip, norm_g, *, batch, seq_len):
    T, W = proj.shape
    nconv = conv_w.shape[1]
    inner = W - nconv
    nc = seq_len // CHUNK
    pad = lambda a: jnp.pad(a.reshape(1, -1), ((0, 0), (0, LANES - a.shape[-1])))
    head_of_lane = jnp.arange(inner, dtype=jnp.int32) // M2_HEAD_DIM
    sel = (jnp.arange(LANES, dtype=jnp.int32)[:, None] == head_of_lane[None, :]).astype(BF16)
    d_x = jnp.repeat(d_skip, M2_HEAD_DIM).reshape(1, inner)
    const = lambda shape: pl.BlockSpec(shape, lambda b, c: (0, 0))
    return pl.pallas_call(
        _m2_kernel, grid=(batch, nc),
        in_specs=[pl.BlockSpec((CHUNK, W), lambda b, c: (b * nc + c, 0)),
                  _prev_rows_spec(W, seq_len),
                  pl.BlockSpec((CHUNK, LANES), lambda b, c: (b * nc + c, 0)),
                  const((CONV_W, nconv)), const((1, nconv)), const((1, LANES)),
                  const((1, LANES)), const((1, inner)), const((1, inner)),
                  const((LANES, inner))],
        out_specs=pl.BlockSpec((CHUNK, inner), lambda b, c: (b * nc + c, 0)),
        out_shape=jax.ShapeDtypeStruct((T, inner), BF16),
        scratch_shapes=[pltpu.VMEM((SUBLANES + CHUNK, nconv), F32),
                        pltpu.VMEM((M2_GROUPS, M2_STATE, inner // M2_GROUPS), F32)],
        compiler_params=_cparams("parallel", "arbitrary"), name="m2_core",
    )(proj, proj, dt_raw, conv_w, conv_b.reshape(1, nconv), pad(dt_bias), pad(a_log),
      d_x, norm_g.reshape(1, inner), sel)


def m2_layer(h, g_mix, w_in, conv_w, conv_b, dt_bias, a_log, d_skip, norm_g, w_out,
             *, batch, seq_len):
    heads = dt_bias.shape[0]
    nmain = w_in.shape[1] - heads
    w_main = w_in[:, :nmain].astype(BF16)
    w_small = jnp.pad(w_in[:, nmain:], ((0, 0), (0, LANES - heads))).astype(BF16)
    proj, dt_raw = norm_matmul(h, g_mix, w_main, w_small)
    y = m2_core(proj, dt_raw, conv_w, conv_b, dt_bias, a_log, d_skip, norm_g,
                batch=batch, seq_len=seq_len)
    return matmul_residual(y, w_out.astype(BF16), h)


def _s5_prep_kernel(lr_ref, li_ref, ldt_ref, br_ref, bi_ref, lbr_ref, lbi_ref, bbr_ref, bbi_ref):
    lr, li = lr_ref[...], li_ref[...]
    dt = jnp.exp(ldt_ref[...])
    mag = jnp.exp(lr * dt)
    lbr = mag * jnp.cos(li * dt)
    lbi = mag * jnp.sin(li * dt)
    nr = lbr - 1.0
    den = lr * lr + li * li
    qr = (nr * lr + lbi * li) / den
    qi = (lbi * lr - nr * li) / den
    br, bi = br_ref[...], bi_ref[...]
    lbr_ref[...] = lbr
    lbi_ref[...] = lbi
    bbr_ref[...] = qr * br - qi * bi
    bbi_ref[...] = qr * bi + qi * br


def _s5_kernel(u_ref, b_ref, cre_ref, cim_ref, lam_ref, d_ref, y_ref, bu_ref, st_ref, *, steps):
    nslab = b_ref.shape[0]
    half = b_ref.shape[2] // 2

    @pl.when(pl.program_id(0) == 0)
    def _():
        st_ref[...] = jnp.zeros_like(st_ref)

    for s in range(nslab):
        us = u_ref[:, s * S5_SLAB:(s + 1) * S5_SLAB].astype(BF16)
        bu_ref[:, 2 * half * s:2 * half * (s + 1)] = _dot(us, b_ref[s])

    for s in range(nslab):
        re = slice(2 * half * s, 2 * half * s + half)
        im = slice(2 * half * s + half, 2 * half * (s + 1))
        lam_r = jnp.broadcast_to(lam_ref[:, re], (SUBLANES, half))
        lam_i = jnp.broadcast_to(lam_ref[:, im], (SUBLANES, half))

        def step(t, carry, re=re, im=im, lam_r=lam_r, lam_i=lam_i):
            xr, xi = carry
            rows = pl.ds(pl.multiple_of(t * SUBLANES, SUBLANES), SUBLANES)
            nr = lam_r * xr - lam_i * xi + bu_ref[rows, re]
            ni = lam_r * xi + lam_i * xr + bu_ref[rows, im]
            bu_ref[rows, re] = nr
            bu_ref[rows, im] = ni
            return nr, ni

        xr, xi = lax.fori_loop(0, steps, step, (st_ref[:, re], st_ref[:, im]))
        st_ref[:, re] = xr
        st_ref[:, im] = xi

    for s in range(nslab):
        re = slice(2 * half * s, 2 * half * s + half)
        im = slice(2 * half * s + half, 2 * half * (s + 1))
        cols = slice(s * S5_SLAB, (s + 1) * S5_SLAB)
        y = (_dot(bu_ref[:, re].astype(BF16), cre_ref[s])
             - _dot(bu_ref[:, im].astype(BF16), cim_ref[s])
             + d_ref[:, cols] * u_ref[:, cols])
        y_ref[:, cols] = jax.nn.gelu(y, approximate=True).astype(y_ref.dtype)


def s5_core(u_tm, lam_re, lam_im, log_dt, b_re, b_im, c_re, c_im, d_skip, *, batch, steps=32):
    assert batch == SUBLANES, "one time step's batch entries fill one sublane tile"
    rows, D = u_tm.shape
    steps = min(steps, rows // batch)
    G, P = lam_re.shape
    K = S5_GROUP
    nslab = D // S5_SLAB
    gps = S5_SLAB // K
    half = gps * P

    rep = lambda a: jnp.repeat(a, K, axis=-1)
    flat = jax.ShapeDtypeStruct((G, P * K), F32)
    lbr, lbi, bbr, bbi = pl.pallas_call(
        _s5_prep_kernel, out_shape=[flat] * 4, name="s5_prep",
    )(rep(lam_re), rep(lam_im), jnp.broadcast_to(log_dt[:, None], (G, P * K)),
      b_re.reshape(G, P * K), b_im.reshape(G, P * K))

    eye = jnp.eye(gps, dtype=F32)
    to_slab = lambda a: a.reshape(G, P, K).transpose(0, 2, 1).reshape(nslab, gps, K, P)
    b_blk = jnp.concatenate(
        [jnp.einsum('sjkp,jJ->sjkJp', to_slab(a), eye).reshape(nslab, S5_SLAB, half)
         for a in (bbr, bbi)], axis=-1).astype(BF16)
    c_blk = [jnp.einsum('sjkp,jJ->sJpjk', a.reshape(nslab, gps, K, P), eye)
             .reshape(nslab, half, S5_SLAB).astype(BF16) for a in (c_re, c_im)]
    lam = jnp.stack([lbr[:, ::K].reshape(nslab, half), lbi[:, ::K].reshape(nslab, half)],
                    axis=1).reshape(1, nslab * 2 * half)

    blk = steps * batch
    const3 = lambda shape: pl.BlockSpec(shape, lambda i: (0, 0, 0))
    return pl.pallas_call(
        functools.partial(_s5_kernel, steps=steps), grid=(rows // blk,),
        in_specs=[pl.BlockSpec((blk, D), lambda i: (i, 0)),
                  const3((nslab, S5_SLAB, 2 * half)),
                  const3((nslab, half, S5_SLAB)), const3((nslab, half, S5_SLAB)),
                  pl.BlockSpec((1, nslab * 2 * half), lambda i: (0, 0)),
                  pl.BlockSpec((1, D), lambda i: (0, 0))],
        out_specs=pl.BlockSpec((blk, D), lambda i: (i, 0)),
        out_shape=jax.ShapeDtypeStruct((rows, D), BF16),
        scratch_shapes=[pltpu.VMEM((blk, nslab * 2 * half), F32),
                        pltpu.VMEM((SUBLANES, nslab * 2 * half), F32)],
        compiler_params=_cparams("arbitrary"), name="s5_core",
    )(u_tm, b_blk, c_blk[0], c_blk[1], lam, d_skip.reshape(1, D))


def s5_layer(h, g_mix, w_in, lam_re, lam_im, log_dt, b_re, b_im, c_re, c_im, d_skip, w_out,
             *, batch, seq_len):
    D = h.shape[1]
    u = norm_matmul(h, g_mix, w_in.astype(BF16), time_major_batch=batch)
    y = s5_core(u.reshape(seq_len * batch, D), lam_re, lam_im, log_dt, b_re, b_im, c_re, c_im,
                d_skip, batch=batch)
    return glu_residual_time_major(y.reshape(seq_len, batch * D), w_out.astype(BF16), h,
                                   batch=batch)


def kernel(x, norm_mix_g, norm_mlp_g, mlp_w1, mlp_w2, gdn_w_in, gdn_conv_w, gdn_a_log, gdn_dt_bias, gdn_o_norm_g, gdn_w_out, s5_w_in, s5_lam_re, s5_lam_im, s5_log_dt, s5_b_re, s5_b_im, s5_c_re, s5_c_im, s5_d, s5_w_out, m2_w_in, m2_conv_w, m2_conv_b, m2_dt_bias, m2_a_log, m2_d, m2_norm_g, m2_w_out, final_norm_g):
    batch, seq_len, d_model = x.shape
    depth = norm_mix_g.shape[0]
    h = x.reshape(batch * seq_len, d_model)
    for i in range(depth):
        kind, j = i % 3, i // 3
        if kind == 0:
            h = gdn_layer(h, norm_mix_g[i], gdn_w_in[j], gdn_conv_w[j], gdn_a_log[j],
                          gdn_dt_bias[j], gdn_o_norm_g[j], gdn_w_out[j],
                          batch=batch, seq_len=seq_len)
        elif kind == 1:
            h = s5_layer(h, norm_mix_g[i], s5_w_in[j], s5_lam_re[j], s5_lam_im[j], s5_log_dt[j],
                         s5_b_re[j], s5_b_im[j], s5_c_re[j], s5_c_im[j], s5_d[j], s5_w_out[j],
                         batch=batch, seq_len=seq_len)
        else:
            h = m2_layer(h, norm_mix_g[i], m2_w_in[j], m2_conv_w[j], m2_conv_b[j], m2_dt_bias[j],
                         m2_a_log[j], m2_d[j], m2_norm_g[j], m2_w_out[j],
                         batch=batch, seq_len=seq_len)
        h = mlp_residual(h, norm_mlp_g[i], mlp_w1[i].astype(BF16), mlp_w2[i].astype(BF16),
                         final_norm_g if i == depth - 1 else None)
    return h.reshape(batch, seq_len, d_model)
```

```python
import functools

import jax
import jax.numpy as jnp
from jax import lax
from jax.experimental import pallas as pl
from jax.experimental.pallas import tpu as pltpu

F32 = jnp.float32
BF16 = jnp.bfloat16

RMS_EPS = 1e-6
L2_EPS = 1e-6
CHUNK = 64
CONV_W = 4
LANES = 128
SUBLANES = 8
HALO = 2 * SUBLANES
CONV_COLS = 512
SEQS_PER_STEP = 4
VMEM_LIMIT = 56 * 1024 * 1024

GDN_HEADS = 8
GDN_DK = 128
GDN_DV = 128
S5_GROUP = 16
S5_STATE = 64
S5_SLAB = 128
M2_HEAD_DIM = 64
M2_STATE = 128
M2_GROUPS = 8


def _cparams(*sem):
    return pltpu.CompilerParams(dimension_semantics=sem, vmem_limit_bytes=VMEM_LIMIT)


def _rms(x, g):
    return x * lax.rsqrt(jnp.mean(x * x, axis=-1, keepdims=True) + RMS_EPS) * g


def _sigmoid(x):
    return 0.5 * (jnp.tanh(0.5 * x) + 1.0)


def _silu(x):
    return x * _sigmoid(x)


def _softplus(x):
    return jnp.maximum(x, 0.0) + jnp.log1p(jnp.exp(-jnp.abs(x)))


def _dot(a, b):
    return jnp.dot(a, b, preferred_element_type=F32)


def _split2(x):
    hi = x.astype(BF16)
    return hi, (x - hi.astype(F32)).astype(BF16)


def _split3(x):
    hi = x.astype(BF16)
    r = x - hi.astype(F32)
    mid = r.astype(BF16)
    return hi, mid, (r - mid.astype(F32)).astype(BF16)


def _dot_nt(a, b):
    return lax.dot_general(a, b, (((1,), (1,)), ((), ())), preferred_element_type=F32)


def _dot_tn(a, b):
    return lax.dot_general(a, b, (((0,), (0,)), ((), ())), preferred_element_type=F32)


def _causal_masks(n):
    row = lax.broadcasted_iota(jnp.int32, (n, n), 0)
    col = lax.broadcasted_iota(jnp.int32, (n, n), 1)
    return row >= col, row > col


def _norm_mm_kernel(x_ref, g_ref, w_ref, o_ref, xn_ref):
    @pl.when(pl.program_id(1) == 0)
    def _():
        xn_ref[...] = _rms(x_ref[...], g_ref[...]).astype(BF16)

    o_ref[...] = _dot(xn_ref[...], w_ref[...]).astype(o_ref.dtype)


def _norm_mm_small_kernel(x_ref, g_ref, w_ref, ws_ref, o_ref, os_ref, xn_ref):
    @pl.when(pl.program_id(1) == 0)
    def _():
        xn = _rms(x_ref[...], g_ref[...]).astype(BF16)
        xn_ref[...] = xn
        os_ref[...] = _dot(xn, ws_ref[...])

    o_ref[...] = _dot(xn_ref[...], w_ref[...]).astype(o_ref.dtype)


def norm_matmul(h, g, w, w_small=None, *, out_dtype, tm=1024, tn=1024):
    T, D = h.shape
    N = w.shape[1]
    tm = min(tm, T)
    grid = (T // tm, N // tn)
    x_spec = pl.BlockSpec((tm, D), lambda i, j: (i, 0))
    g_spec = pl.BlockSpec((1, D), lambda i, j: (0, 0))
    w_spec = pl.BlockSpec((D, tn), lambda i, j: (0, j))
    out_shape = jax.ShapeDtypeStruct((T, N), out_dtype)
    o_spec = pl.BlockSpec((tm, tn), lambda i, j: (i, j))
    scratch = [pltpu.VMEM((tm, D), BF16)]
    g2 = g.reshape(1, D)
    if w_small is None:
        return pl.pallas_call(
            _norm_mm_kernel, grid=grid, in_specs=[x_spec, g_spec, w_spec],
            out_specs=o_spec, out_shape=out_shape, scratch_shapes=scratch,
            compiler_params=_cparams("parallel", "arbitrary"), name="norm_matmul",
        )(h, g2, w)
    ns = w_small.shape[1]
    return pl.pallas_call(
        _norm_mm_small_kernel, grid=grid,
        in_specs=[x_spec, g_spec, w_spec, pl.BlockSpec((D, ns), lambda i, j: (0, 0))],
        out_specs=[o_spec, pl.BlockSpec((tm, ns), lambda i, j: (i, 0))],
        out_shape=[out_shape, jax.ShapeDtypeStruct((T, ns), F32)],
        scratch_shapes=scratch,
        compiler_params=_cparams("parallel", "arbitrary"), name="norm_matmul_small",
    )(h, g2, w, w_small)


def _mm_res_kernel(x_ref, w_ref, r_ref, o_ref):
    o_ref[...] = r_ref[...] + _dot(x_ref[...], w_ref[...])


def matmul_residual(x, w, res, *, tm=1024):
    T, K = x.shape
    N = w.shape[1]
    tm = min(tm, T)
    return pl.pallas_call(
        _mm_res_kernel, grid=(T // tm,),
        in_specs=[pl.BlockSpec((tm, K), lambda i: (i, 0)),
                  pl.BlockSpec((K, N), lambda i: (0, 0)),
                  pl.BlockSpec((tm, N), lambda i: (i, 0))],
        out_specs=pl.BlockSpec((tm, N), lambda i: (i, 0)),
        out_shape=jax.ShapeDtypeStruct((T, N), F32),
        compiler_params=_cparams("parallel"), name="matmul_residual",
    )(x, w, res)


def _glu_res_kernel(x_ref, w_ref, r_ref, o_ref):
    n = o_ref.shape[1]
    ag = _dot(x_ref[...], w_ref[...])
    o_ref[...] = r_ref[...] + ag[:, :n] * _sigmoid(ag[:, n:])


def glu_residual(y, w, res, *, tm=1024):
    T, K = y.shape
    N = w.shape[1] // 2
    tm = min(tm, T)
    return pl.pallas_call(
        _glu_res_kernel, grid=(T // tm,),
        in_specs=[pl.BlockSpec((tm, K), lambda i: (i, 0)),
                  pl.BlockSpec((K, 2 * N), lambda i: (0, 0)),
                  pl.BlockSpec((tm, N), lambda i: (i, 0))],
        out_specs=pl.BlockSpec((tm, N), lambda i: (i, 0)),
        out_shape=jax.ShapeDtypeStruct((T, N), F32),
        compiler_params=_cparams("parallel"), name="glu_residual",
    )(y, w, res)


def _mlp_kernel(x_ref, g_ref, w1_ref, w2_ref, *rest, final):
    if final:
        gf_ref, o_ref, xn_ref, acc_ref = rest
    else:
        o_ref, xn_ref, acc_ref = rest
    j = pl.program_id(1)

    @pl.when(j == 0)
    def _():
        xn_ref[...] = _rms(x_ref[...], g_ref[...]).astype(BF16)
        acc_ref[...] = jnp.zeros_like(acc_ref)

    a = jnp.maximum(_dot(xn_ref[...], w1_ref[...]), 0.0)
    acc_ref[...] += _dot((a * a).astype(BF16), w2_ref[...])

    @pl.when(j == pl.num_programs(1) - 1)
    def _():
        y = x_ref[...] + acc_ref[...]
        o_ref[...] = _rms(y, gf_ref[...]) if final else y


def mlp_residual(h, g, w1, w2, g_final=None, *, tm=1024, tf=512):
    T, D = h.shape
    FF = w1.shape[1]
    tm = min(tm, T)
    final = g_final is not None
    in_specs = [pl.BlockSpec((tm, D), lambda i, j: (i, 0)),
                pl.BlockSpec((1, D), lambda i, j: (0, 0)),
                pl.BlockSpec((D, tf), lambda i, j: (0, j)),
                pl.BlockSpec((tf, D), lambda i, j: (j, 0))]
    args = [h, g.reshape(1, D), w1, w2]
    if final:
        in_specs.append(pl.BlockSpec((1, D), lambda i, j: (0, 0)))
        args.append(g_final.reshape(1, D))
    return pl.pallas_call(
        functools.partial(_mlp_kernel, final=final), grid=(T // tm, FF // tf),
        in_specs=in_specs,
        out_specs=pl.BlockSpec((tm, D), lambda i, j: (i, 0)),
        out_shape=jax.ShapeDtypeStruct((T, D), F32),
        scratch_shapes=[pltpu.VMEM((tm, D), BF16), pltpu.VMEM((tm, D), F32)],
        compiler_params=_cparams("parallel", "arbitrary"), name="mlp_residual",
    )(*args)


def _conv_window(cur, prev, xb_ref, first_chunk):
    xb_ref[0:HALO, :] = jnp.where(first_chunk, jnp.zeros_like(prev), prev)
    xb_ref[HALO:HALO + CHUNK, :] = cur


def _causal_conv(xb_ref, cw_ref):
    i = lax.broadcasted_iota(jnp.int32, (CHUNK, HALO + CHUNK), 0)
    c = lax.broadcasted_iota(jnp.int32, (CHUNK, HALO + CHUNK), 1)
    shift = jnp.concatenate(
        [jnp.where(c == i + (HALO - (CONV_W - 1) + w), 1.0, 0.0) for w in range(CONV_W - 1)],
        axis=0).astype(BF16)
    outs = []
    for c0 in range(0, xb_ref.shape[1], CONV_COLS):
        cols = slice(c0, c0 + CONV_COLS)
        shifted = _dot(shift, xb_ref[:, cols])
        acc = xb_ref[HALO:HALO + CHUNK, cols].astype(F32) * cw_ref[CONV_W - 1:CONV_W, cols]
        for w in range(CONV_W - 1):
            acc = acc + shifted[w * CHUNK:(w + 1) * CHUNK] * cw_ref[w:w + 1, cols]
        outs.append(acc)
    return jnp.concatenate(outs, axis=1)


def _chunk_cumsum(x, causal):
    n = x.shape[1]
    parts = _dot(jnp.where(causal, 1.0, 0.0).astype(BF16), jnp.concatenate(_split3(x), axis=1))
    return parts[:, :n] + (parts[:, n:2 * n] + parts[:, 2 * n:])


def _prev_rows_spec(width):
    blocks_per_chunk = CHUNK // HALO
    return pl.BlockSpec(
        (SEQS_PER_STEP, HALO, width),
        lambda b, c: (b, jnp.maximum(c * blocks_per_chunk - 1, 0), 0))


def _dot_split(a, b):
    return _dot(jnp.concatenate([a[0], a[0], a[1]], axis=1),
                jnp.concatenate([b[0], b[1], b[0]], axis=0))


def _unit_lower_inverses(mats, nilpotent):
    n = mats[0].shape[0]
    row = lax.broadcasted_iota(jnp.int32, (n, n), 0)
    col = lax.broadcasted_iota(jnp.int32, (n, n), 1)
    eye = jnp.where(row == col, 1.0, 0.0)
    ps = [eye - a for a in mats]
    aks = [a.astype(BF16) for a in mats]
    power = 2
    while power < nilpotent:
        aks = [_dot(ak, ak).astype(BF16) for ak in aks]
        ps = [p + _dot(p.astype(BF16), ak) for p, ak in zip(ps, aks)]
        power *= 2
    resid = [eye - _dot_split(_split2(eye + a), _split2(p)) for a, p in zip(mats, ps)]
    return [p + _dot(p.astype(BF16), r.astype(BF16)) for p, r in zip(ps, resid)]


def _gdn_kernel(cur_ref, prev_ref, ab_ref, cw_ref, alog_ref, dtb_ref, ong_ref,
                o_ref, xb_ref, s_ref):
    c = pl.program_id(1)
    nseq = cur_ref.shape[0]
    nqk = GDN_HEADS * GDN_DK
    nconv = 2 * nqk + GDN_HEADS * GDN_DV
    units = [(b, p) for b in range(nseq) for p in range(GDN_HEADS // 2)]
    nu = range(len(units))
    stack = lambda f, p: jnp.concatenate([f(2 * p), f(2 * p + 1)], axis=0)

    @pl.when(c == 0)
    def _():
        s_ref[...] = jnp.zeros_like(s_ref)

    for b in range(nseq):
        _conv_window(cur_ref[b, :, :nconv], prev_ref[b, :, :nconv], xb_ref.at[b], c == 0)
    qkv = [_silu(_causal_conv(xb_ref.at[b], cw_ref)) for b in range(nseq)]

    causal, _ = _causal_masks(CHUNK)
    row = lax.broadcasted_iota(jnp.int32, (2 * CHUNK, 2 * CHUNK), 0)
    lane = lax.broadcasted_iota(jnp.int32, (2 * CHUNK, 2 * CHUNK), 1)
    first = lane < CHUNK
    same_head = (row < CHUNK) == first
    t_row = jnp.where(row < CHUNK, row, row - CHUNK)
    t_col = jnp.where(first, lane, lane - CHUNK)
    causal2 = same_head & (t_row >= t_col)
    strict2 = same_head & (t_row > t_col)

    ab = [ab_ref[b] for b in range(nseq)]
    beta_all = [_sigmoid(t) for t in ab]
    G = [_chunk_cumsum(-jnp.exp(alog_ref[...]) * _softplus(t + dtb_ref[...]), causal) for t in ab]
    G_rows = [jnp.concatenate([t, t], axis=0).T for t in G]
    eG = [jnp.exp(t) for t in G]
    eG_rev = [jnp.exp(t[CHUNK - 1:CHUNK, :] - t) for t in G]

    def heads_of(width, offset, b, p):
        return stack(lambda h: qkv[b][:, offset + h * width:offset + (h + 1) * width], p)

    q = [heads_of(GDN_DK, 0, b, p) for b, p in units]
    k = [heads_of(GDN_DK, nqk, b, p) for b, p in units]
    v = [heads_of(GDN_DV, 2 * nqk, b, p) for b, p in units]
    sq_hi, sq_lo = _split2(jnp.concatenate([t * t for t in q + k], axis=0))
    ones = jnp.ones((GDN_DK, GDN_DK), BF16)
    inv_norm = lax.rsqrt(_dot(sq_hi, ones) + _dot(sq_lo, ones) + L2_EPS)
    rows = 2 * CHUNK
    q = [q[u] * inv_norm[u * rows:(u + 1) * rows] * (GDN_DK ** -0.5) for u in nu]
    k = [k[u] * inv_norm[(len(units) + u) * rows:(len(units) + u + 1) * rows] for u in nu]
    col_of = lambda a, b, p, off=0: stack(lambda h: a[b][:, off + h:off + h + 1], p)
    beta = [col_of(beta_all, b, p, GDN_HEADS) for b, p in units]
    eg = [col_of(eG, b, p) for b, p in units]
    decay = []
    for b, p in units:
        g_row = jnp.where(first[0:1], G_rows[b][2 * p:2 * p + 1, :], G_rows[b][2 * p + 1:2 * p + 2, :])
        diff = col_of(G, b, p) - g_row
        decay.append(jnp.where(causal2, jnp.exp(jnp.where(causal2, diff, 0.0)), 0.0))
    kb = [t.astype(BF16) for t in k]
    qkk = [_dot_nt(jnp.concatenate([q[u].astype(BF16), kb[u]], axis=0), kb[u]) for u in nu]
    t_inv = _unit_lower_inverses(
        [jnp.where(strict2, beta[u] * qkk[u][rows:] * decay[u], 0.0) for u in nu], CHUNK)
    uw = [_dot(t_inv[u].astype(BF16),
               jnp.concatenate([v[u] * beta[u], k[u] * (beta[u] * eg[u])], axis=1).astype(BF16))
          for u in nu]
    qe = [q[u] * eg[u] for u in nu]
    halves = (slice(0, CHUNK), slice(CHUNK, rows))
    S = [[s_ref[b, 2 * p + i] for i in range(2)] for b, p in units]
    Sb = [[t.astype(BF16) for t in pair] for pair in S]
    ws = [[_dot(jnp.concatenate([uw[u][halves[i], GDN_DV:], qe[u][halves[i]]], axis=0).astype(BF16),
                Sb[u][i]) for i in range(2)] for u in nu]
    vb = [(uw[u][:, :GDN_DV] - jnp.concatenate([ws[u][0][:CHUNK], ws[u][1][:CHUNK]], axis=0))
          .astype(BF16) for u in nu]
    o = [jnp.concatenate([ws[u][0][CHUNK:], ws[u][1][CHUNK:]], axis=0)
         + _dot((qkk[u][:rows] * decay[u]).astype(BF16), vb[u]) for u in nu]
    k_dec = [(k[u] * col_of(eG_rev, b, p)).astype(BF16) for u, (b, p) in enumerate(units)]
    s_new = [[eG[b][CHUNK - 1:CHUNK, 2 * p + i:2 * p + i + 1] * S[u][i]
              + _dot_tn(k_dec[u][halves[i]], vb[u][halves[i]]) for i in range(2)]
             for u, (b, p) in enumerate(units)]
    for u, (b, p) in enumerate(units):
        for i in range(2):
            s_ref[b, 2 * p + i] = s_new[u][i]
    for u, (b, p) in enumerate(units):
        for i in range(2):
            h = 2 * p + i
            gate = cur_ref[b, :, nconv + h * GDN_DV:nconv + (h + 1) * GDN_DV].astype(F32)
            o_ref[b, :, h * GDN_DV:(h + 1) * GDN_DV] = (
                _rms(o[u][halves[i]], ong_ref[...]) * _silu(gate)).astype(o_ref.dtype)


def gdn_core(proj, ab, conv_w, a_log, dt_bias, o_norm_g):
    batch, seq_len, W = proj.shape
    nv = GDN_HEADS * GDN_DV
    nconv = conv_w.shape[1]
    nb = SEQS_PER_STEP
    pad = lambda a: jnp.pad(a.reshape(1, -1), ((0, 0), (0, LANES - a.shape[-1])))
    const = lambda shape: pl.BlockSpec(shape, lambda b, c: (0, 0))
    return pl.pallas_call(
        _gdn_kernel, grid=(batch // nb, seq_len // CHUNK),
        in_specs=[pl.BlockSpec((nb, CHUNK, W), lambda b, c: (b, c, 0)),
                  _prev_rows_spec(W),
                  pl.BlockSpec((nb, CHUNK, LANES), lambda b, c: (b, c, 0)),
                  const((CONV_W, nconv)), const((1, LANES)), const((1, LANES)),
                  const((1, GDN_DV))],
        out_specs=pl.BlockSpec((nb, CHUNK, nv), lambda b, c: (b, c, 0)),
        out_shape=jax.ShapeDtypeStruct((batch, seq_len, nv), BF16),
        scratch_shapes=[pltpu.VMEM((nb, HALO + CHUNK, nconv), BF16),
                        pltpu.VMEM((nb, GDN_HEADS, GDN_DK, GDN_DV), F32)],
        compiler_params=_cparams("parallel", "arbitrary"), name="gdn_core",
    )(proj, proj, ab, conv_w, pad(a_log), pad(dt_bias), o_norm_g.reshape(1, GDN_DV))


def gdn_layer(h, g_mix, w_in, conv_w, a_log, dt_bias, o_norm_g, w_out, *, batch, seq_len):
    nmain = conv_w.shape[1] + GDN_HEADS * GDN_DV
    w_main = w_in[:, :nmain].astype(BF16)
    w_small = jnp.pad(w_in[:, nmain:], ((0, 0), (0, LANES - 2 * GDN_HEADS))).astype(BF16)
    proj, ab = norm_matmul(h, g_mix, w_main, w_small, out_dtype=BF16)
    o = gdn_core(proj.reshape(batch, seq_len, -1), ab.reshape(batch, seq_len, -1),
                 conv_w, a_log, dt_bias, o_norm_g)
    return matmul_residual(o.reshape(batch * seq_len, -1), w_out.astype(BF16), h)


def _m2_kernel(cur_ref, prev_ref, dt_ref, cw_ref, cb_ref, dtb_ref, alog_ref, dsk_ref,
               ng_ref, sel_ref, o_ref, xb_ref, s_ref):
    c = pl.program_id(1)
    nseq = cur_ref.shape[0]
    seqs = range(nseq)
    inner = dsk_ref.shape[1]
    nbc = M2_GROUPS * M2_STATE
    pair_w = 2 * M2_HEAD_DIM
    pairs_per_group = inner // M2_GROUPS // pair_w
    gw = inner // M2_GROUPS

    @pl.when(c == 0)
    def _():
        s_ref[...] = jnp.zeros_like(s_ref)

    for b in seqs:
        _conv_window(cur_ref[b, :, inner:], prev_ref[b, :, inner:], xb_ref.at[b], c == 0)
    xbc = [_silu(_causal_conv(xb_ref.at[b], cw_ref) + cb_ref[...]) for b in seqs]
    x = [t[:, :inner] for t in xbc]

    causal, _ = _causal_masks(CHUNK)
    dt = [_softplus(dt_ref[b] + dtb_ref[...]) for b in seqs]
    cum = [_chunk_cumsum(t * (-jnp.exp(alog_ref[...])), causal) for t in dt]
    cum_rows = [jnp.concatenate([t, t], axis=0).T for t in cum]
    per_head = jnp.concatenate(
        [a for b in seqs
         for a in (dt[b], cum[b], jnp.exp(cum[b]), jnp.exp(cum[b][CHUNK - 1:CHUNK, :] - cum[b]))],
        axis=0)
    rows = per_head.shape[0]
    parts = _dot(jnp.concatenate(_split2(per_head), axis=0), sel_ref[...])
    spread = parts[:rows] + parts[rows:]
    dt_x = [spread[(4 * b) * CHUNK:(4 * b + 1) * CHUNK] for b in seqs]
    cum_x = [spread[(4 * b + 1) * CHUNK:(4 * b + 2) * CHUNK] for b in seqs]
    ecum = [spread[(4 * b + 2) * CHUNK:(4 * b + 3) * CHUNK] for b in seqs]
    xdt = [x[b] * dt_x[b] for b in seqs]
    xd = [xdt[b] * spread[(4 * b + 3) * CHUNK:(4 * b + 4) * CHUNK] for b in seqs]

    lane = lax.broadcasted_iota(jnp.int32, (CHUNK, pair_w), 1)
    row = lax.broadcasted_iota(jnp.int32, (CHUNK, pair_w), 0)
    first = lane < M2_HEAD_DIM
    causal2 = row >= jnp.where(first, lane, lane - M2_HEAD_DIM)

    units = [(b, g) for b in seqs for g in range(M2_GROUPS)]
    gcol = lambda g: slice(g * gw, (g + 1) * gw)
    Bg = [xbc[b][:, inner + g * M2_STATE:inner + (g + 1) * M2_STATE].astype(BF16) for b, g in units]
    Cg = [xbc[b][:, inner + nbc + g * M2_STATE:inner + nbc + (g + 1) * M2_STATE].astype(BF16)
          for b, g in units]
    cb2 = [_dot_nt(Cg[u], jnp.concatenate([Bg[u], Bg[u]], axis=0))
           for u in range(len(units))]
    S = [s_ref[b, g] for b, g in units]
    y_off = [_dot(Cg[u], S[u].astype(BF16)) * ecum[b][:, gcol(g)] for u, (b, g) in enumerate(units)]
    s_new = [ecum[b][CHUNK - 1:CHUNK, gcol(g)] * S[u]
             + _dot_tn(Bg[u], xd[b][:, gcol(g)].astype(BF16)) for u, (b, g) in enumerate(units)]
    for u, (b, g) in enumerate(units):
        s_ref[b, g] = s_new[u]
    y_diag = []
    for u, (b, g) in enumerate(units):
        for m in range(pairs_per_group):
            p = g * pairs_per_group + m
            cols = slice(p * pair_w, (p + 1) * pair_w)
            h1 = 2 * p
            row_pair = jnp.where(first[0:1], cum_rows[b][h1:h1 + 1, :], cum_rows[b][h1 + 1:h1 + 2, :])
            diff = cum_x[b][:, cols] - row_pair
            lmat = jnp.where(causal2, jnp.exp(jnp.where(causal2, diff, 0.0)), 0.0)
            xp = xdt[b][:, cols]
            x_bd = jnp.concatenate([jnp.where(first, xp, 0.0), jnp.where(first, 0.0, xp)], axis=0)
            y_diag.append(_dot((cb2[u] * lmat).astype(BF16), x_bd.astype(BF16)))
    for u, (b, g) in enumerate(units):
        cols = gcol(g)
        yd = jnp.concatenate(y_diag[u * pairs_per_group:(u + 1) * pairs_per_group], axis=1)
        y = ((yd + y_off[u] + dsk_ref[:, cols] * x[b][:, cols])
             * _silu(cur_ref[b, :, cols].astype(F32)))
        o_ref[b, :, cols] = _rms(y, ng_ref[:, cols]).astype(o_ref.dtype)


def m2_core(proj, dt_raw, conv_w, conv_b, dt_bias, a_log, d_skip, norm_g):
    batch, seq_len, W = proj.shape
    nconv = conv_w.shape[1]
    inner = W - nconv
    nb = SEQS_PER_STEP
    pad = lambda a: jnp.pad(a.reshape(1, -1), ((0, 0), (0, LANES - a.shape[-1])))
    head_of_lane = jnp.arange(inner, dtype=jnp.int32) // M2_HEAD_DIM
    sel = (jnp.arange(LANES, dtype=jnp.int32)[:, None] == head_of_lane[None, :]).astype(BF16)
    d_x = jnp.repeat(d_skip, M2_HEAD_DIM).reshape(1, inner)
    const = lambda shape: pl.BlockSpec(shape, lambda b, c: (0, 0))
    return pl.pallas_call(
        _m2_kernel, grid=(batch // nb, seq_len // CHUNK),
        in_specs=[pl.BlockSpec((nb, CHUNK, W), lambda b, c: (b, c, 0)),
                  _prev_rows_spec(W),
                  pl.BlockSpec((nb, CHUNK, LANES), lambda b, c: (b, c, 0)),
                  const((CONV_W, nconv)), const((1, nconv)), const((1, LANES)),
                  const((1, LANES)), const((1, inner)), const((1, inner)),
                  const((LANES, inner))],
        out_specs=pl.BlockSpec((nb, CHUNK, inner), lambda b, c: (b, c, 0)),
        out_shape=jax.ShapeDtypeStruct((batch, seq_len, inner), BF16),
        scratch_shapes=[pltpu.VMEM((nb, HALO + CHUNK, nconv), BF16),
                        pltpu.VMEM((nb, M2_GROUPS, M2_STATE, inner // M2_GROUPS), F32)],
        compiler_params=_cparams("parallel", "arbitrary"), name="m2_core",
    )(proj, proj, dt_raw, conv_w, conv_b.reshape(1, nconv), pad(dt_bias), pad(a_log),
      d_x, norm_g.reshape(1, inner), sel)


def m2_layer(h, g_mix, w_in, conv_w, conv_b, dt_bias, a_log, d_skip, norm_g, w_out,
             *, batch, seq_len):
    heads = dt_bias.shape[0]
    nmain = w_in.shape[1] - heads
    w_main = w_in[:, :nmain].astype(BF16)
    w_small = jnp.pad(w_in[:, nmain:], ((0, 0), (0, LANES - heads))).astype(BF16)
    proj, dt_raw = norm_matmul(h, g_mix, w_main, w_small, out_dtype=BF16)
    y = m2_core(proj.reshape(batch, seq_len, -1), dt_raw.reshape(batch, seq_len, -1),
                conv_w, conv_b, dt_bias, a_log, d_skip, norm_g)
    return matmul_residual(y.reshape(batch * seq_len, -1), w_out.astype(BF16), h)


def _s5_prep_kernel(lr_ref, li_ref, ldt_ref, br_ref, bi_ref, lbr_ref, lbi_ref, bbr_ref, bbi_ref):
    lr, li = lr_ref[...], li_ref[...]
    dt = jnp.exp(ldt_ref[...])
    mag = jnp.exp(lr * dt)
    lbr = mag * jnp.cos(li * dt)
    lbi = mag * jnp.sin(li * dt)
    nr = lbr - 1.0
    den = lr * lr + li * li
    qr = (nr * lr + lbi * li) / den
    qi = (lbi * lr - nr * li) / den
    br, bi = br_ref[...], bi_ref[...]
    lbr_ref[...] = lbr
    lbi_ref[...] = lbi
    bbr_ref[...] = qr * br - qi * bi
    bbi_ref[...] = qr * bi + qi * br


def _s5_kernel(u_ref, b_ref, cre_ref, cim_ref, lam_ref, d_ref, y_ref, bu_ref, st_ref, *, steps):
    nslab = b_ref.shape[0]
    half = b_ref.shape[2] // 2

    @pl.when(pl.program_id(0) == 0)
    def _():
        st_ref[...] = jnp.zeros_like(st_ref)

    batch = u_ref.shape[0]
    rows = batch * steps

    def slab_u(s):
        return u_ref[:, :, s * S5_SLAB:(s + 1) * S5_SLAB].reshape(rows, S5_SLAB)

    nb = half // LANES

    for s in range(nslab):
        bu = _dot(slab_u(s).astype(BF16), b_ref[s])
        for j in range(2 * nb):
            bu_ref[2 * nb * s + j] = bu[:, j * LANES:(j + 1) * LANES]

    for s in range(nslab):
        re = slice(2 * half * s, 2 * half * s + half)
        im = slice(2 * half * s + half, 2 * half * (s + 1))
        lam_r = jnp.broadcast_to(lam_ref[:, re], (batch, half))
        lam_i = jnp.broadcast_to(lam_ref[:, im], (batch, half))

        def step(t, carry, s=s, lam_r=lam_r, lam_i=lam_i):
            xr, xi = carry
            at_t = pl.ds(t, batch, stride=steps)
            bu_r = jnp.concatenate([bu_ref[2 * nb * s + j, at_t, :] for j in range(nb)], axis=1)
            bu_i = jnp.concatenate([bu_ref[2 * nb * s + nb + j, at_t, :] for j in range(nb)], axis=1)
            nr = lam_r * xr - lam_i * xi + bu_r
            ni = lam_r * xi + lam_i * xr + bu_i
            for j in range(nb):
                bu_ref[2 * nb * s + j, at_t, :] = nr[:, j * LANES:(j + 1) * LANES]
                bu_ref[2 * nb * s + nb + j, at_t, :] = ni[:, j * LANES:(j + 1) * LANES]
            return nr, ni

        xr, xi = lax.fori_loop(0, steps, step, (st_ref[:, re], st_ref[:, im]))
        st_ref[:, re] = xr
        st_ref[:, im] = xi

    for s in range(nslab):
        cols = slice(s * S5_SLAB, (s + 1) * S5_SLAB)
        x_re = jnp.concatenate([bu_ref[2 * nb * s + j] for j in range(nb)], axis=1)
        x_im = jnp.concatenate([bu_ref[2 * nb * s + nb + j] for j in range(nb)], axis=1)
        y = (_dot(x_re.astype(BF16), cre_ref[s]) - _dot(x_im.astype(BF16), cim_ref[s])
             + d_ref[:, cols] * slab_u(s))
        y_ref[:, :, cols] = (jax.nn.gelu(y, approximate=True).astype(y_ref.dtype)
                             .reshape(batch, steps, S5_SLAB))


def s5_core(u, lam_re, lam_im, log_dt, b_re, b_im, c_re, c_im, d_skip, *, steps=32):
    batch, L, D = u.shape
    assert batch == SUBLANES, "one time step's batch entries fill one sublane tile"
    steps = min(steps, L)
    G, P = lam_re.shape
    K = S5_GROUP
    nslab = D // S5_SLAB
    gps = S5_SLAB // K
    half = gps * P

    rep = lambda a: jnp.repeat(a, K, axis=-1)
    flat = jax.ShapeDtypeStruct((G, P * K), F32)
    lbr, lbi, bbr, bbi = pl.pallas_call(
        _s5_prep_kernel, out_shape=[flat] * 4, name="s5_prep",
    )(rep(lam_re), rep(lam_im), jnp.broadcast_to(log_dt[:, None], (G, P * K)),
      b_re.reshape(G, P * K), b_im.reshape(G, P * K))

    eye = jnp.eye(gps, dtype=F32)
    to_slab = lambda a: a.reshape(G, P, K).transpose(0, 2, 1).reshape(nslab, gps, K, P)
    b_blk = jnp.concatenate(
        [jnp.einsum('sjkp,jJ->sjkJp', to_slab(a), eye).reshape(nslab, S5_SLAB, half)
         for a in (bbr, bbi)], axis=-1).astype(BF16)
    c_blk = [jnp.einsum('sjkp,jJ->sJpjk', a.reshape(nslab, gps, K, P), eye)
             .reshape(nslab, half, S5_SLAB).astype(BF16) for a in (c_re, c_im)]
    lam = jnp.stack([lbr[:, ::K].reshape(nslab, half), lbi[:, ::K].reshape(nslab, half)],
                    axis=1).reshape(1, nslab * 2 * half)

    const3 = lambda shape: pl.BlockSpec(shape, lambda i: (0, 0, 0))
    return pl.pallas_call(
        functools.partial(_s5_kernel, steps=steps), grid=(L // steps,),
        in_specs=[pl.BlockSpec((batch, steps, D), lambda i: (0, i, 0)),
                  const3((nslab, S5_SLAB, 2 * half)),
                  const3((nslab, half, S5_SLAB)), const3((nslab, half, S5_SLAB)),
                  pl.BlockSpec((1, nslab * 2 * half), lambda i: (0, 0)),
                  pl.BlockSpec((1, D), lambda i: (0, 0))],
        out_specs=pl.BlockSpec((batch, steps, D), lambda i: (0, i, 0)),
        out_shape=jax.ShapeDtypeStruct((batch, L, D), BF16),
        scratch_shapes=[pltpu.VMEM((nslab * 2 * half // LANES, batch * steps, LANES), F32),
                        pltpu.VMEM((batch, nslab * 2 * half), F32)],
        compiler_params=_cparams("arbitrary"), name="s5_core",
    )(u, b_blk, c_blk[0], c_blk[1], lam, d_skip.reshape(1, D))


def s5_layer(h, g_mix, w_in, lam_re, lam_im, log_dt, b_re, b_im, c_re, c_im, d_skip, w_out,
             *, batch, seq_len):
    T, D = h.shape
    u = norm_matmul(h, g_mix, w_in.astype(BF16), out_dtype=F32)
    y = s5_core(u.reshape(batch, seq_len, D), lam_re, lam_im, log_dt, b_re, b_im, c_re, c_im,
                d_skip)
    return glu_residual(y.reshape(T, D), w_out.astype(BF16), h)


def kernel(x, norm_mix_g, norm_mlp_g, mlp_w1, mlp_w2, gdn_w_in, gdn_conv_w, gdn_a_log, gdn_dt_bias, gdn_o_norm_g, gdn_w_out, s5_w_in, s5_lam_re, s5_lam_im, s5_log_dt, s5_b_re, s5_b_im, s5_c_re, s5_c_im, s5_d, s5_w_out, m2_w_in, m2_conv_w, m2_conv_b, m2_dt_bias, m2_a_log, m2_d, m2_norm_g, m2_w_out, final_norm_g):
    batch, seq_len, d_model = x.shape
    depth = norm_mix_g.shape[0]
    h = x.reshape(batch * seq_len, d_model)
    for i in range(depth):
        kind, j = i % 3, i // 3
        if kind == 0:
            h = gdn_layer(h, norm_mix_g[i], gdn_w_in[j], gdn_conv_w[j], gdn_a_log[j],
                          gdn_dt_bias[j], gdn_o_norm_g[j], gdn_w_out[j],
                          batch=batch, seq_len=seq_len)
        elif kind == 1:
            h = s5_layer(h, norm_mix_g[i], s5_w_in[j], s5_lam_re[j], s5_lam_im[j], s5_log_dt[j],
                         s5_b_re[j], s5_b_im[j], s5_c_re[j], s5_c_im[j], s5_d[j], s5_w_out[j],
                         batch=batch, seq_len=seq_len)
        else:
            h = m2_layer(h, norm_mix_g[i], m2_w_in[j], m2_conv_w[j], m2_conv_b[j], m2_dt_bias[j],
                         m2_a_log[j], m2_d[j], m2_norm_g[j], m2_w_out[j],
                         batch=batch, seq_len=seq_len)
        h = mlp_residual(h, norm_mlp_g[i], mlp_w1[i].astype(BF16), mlp_w2[i].astype(BF16),
                         final_norm_g if i == depth - 1 else None)
    return h.reshape(batch, seq_len, d_model)
```

```python
import functools

import jax
import jax.numpy as jnp
from jax import lax
from jax.experimental import pallas as pl
from jax.experimental.pallas import tpu as pltpu

F32 = jnp.float32
BF16 = jnp.bfloat16

RMS_EPS = 1e-6
L2_EPS = 1e-6
CHUNK = 64
CONV_W = 4
LANES = 128
SUBLANES = 8
HALO = 2 * SUBLANES
CONV_COLS = 512
SEQS_PER_STEP = 4
VMEM_LIMIT = 56 * 1024 * 1024

GDN_HEADS = 8
GDN_DK = 128
GDN_DV = 128
S5_GROUP = 16
S5_STATE = 64
S5_SLAB = 128
M2_HEAD_DIM = 64
M2_STATE = 128
M2_GROUPS = 8


def _cparams(*sem):
    return pltpu.CompilerParams(dimension_semantics=sem, vmem_limit_bytes=VMEM_LIMIT)


def _rms(x, g):
    return x * lax.rsqrt(jnp.mean(x * x, axis=-1, keepdims=True) + RMS_EPS) * g


def _sigmoid(x):
    return 0.5 * (jnp.tanh(0.5 * x) + 1.0)


def _silu(x):
    return x * _sigmoid(x)


def _softplus(x):
    return jnp.maximum(x, 0.0) + jnp.log1p(jnp.exp(-jnp.abs(x)))


def _dot(a, b):
    return jnp.dot(a, b, preferred_element_type=F32)


def _split2(x):
    hi = x.astype(BF16)
    return hi, (x - hi.astype(F32)).astype(BF16)


def _split3(x):
    hi = x.astype(BF16)
    r = x - hi.astype(F32)
    mid = r.astype(BF16)
    return hi, mid, (r - mid.astype(F32)).astype(BF16)


def _dot_nt(a, b):
    return lax.dot_general(a, b, (((1,), (1,)), ((), ())), preferred_element_type=F32)


def _dot_tn(a, b):
    return lax.dot_general(a, b, (((0,), (0,)), ((), ())), preferred_element_type=F32)


def _causal_masks(n):
    row = lax.broadcasted_iota(jnp.int32, (n, n), 0)
    col = lax.broadcasted_iota(jnp.int32, (n, n), 1)
    return row >= col, row > col


def _norm_mm_kernel(x_ref, g_ref, w_ref, o_ref, xn_ref):
    @pl.when(pl.program_id(1) == 0)
    def _():
        xn_ref[...] = _rms(x_ref[...], g_ref[...]).astype(BF16)

    o_ref[...] = _dot(xn_ref[...], w_ref[...].astype(BF16)).astype(o_ref.dtype)


def _norm_mm_small_kernel(x_ref, g_ref, w_ref, ws_ref, o_ref, os_ref, xn_ref):
    @pl.when(pl.program_id(1) == 0)
    def _():
        xn = _rms(x_ref[...], g_ref[...]).astype(BF16)
        xn_ref[...] = xn
        os_ref[...] = _dot(xn, ws_ref[...])

    o_ref[...] = _dot(xn_ref[...], w_ref[...].astype(BF16)).astype(o_ref.dtype)


def norm_matmul(h, g, w_stack, layer, n_out, w_small=None, *, out_dtype, tm=1024, tn=1024):
    T, D = h.shape
    N = n_out
    tm = min(tm, T)
    grid = (T // tm, N // tn)
    x_spec = pl.BlockSpec((tm, D), lambda i, j: (i, 0))
    g_spec = pl.BlockSpec((1, D), lambda i, j: (0, 0))
    w_spec = pl.BlockSpec((None, D, tn), lambda i, j: (layer, 0, j))
    w = w_stack
    out_shape = jax.ShapeDtypeStruct((T, N), out_dtype)
    o_spec = pl.BlockSpec((tm, tn), lambda i, j: (i, j))
    scratch = [pltpu.VMEM((tm, D), BF16)]
    g2 = g.reshape(1, D)
    if w_small is None:
        return pl.pallas_call(
            _norm_mm_kernel, grid=grid, in_specs=[x_spec, g_spec, w_spec],
            out_specs=o_spec, out_shape=out_shape, scratch_shapes=scratch,
            compiler_params=_cparams("parallel", "arbitrary"), name="norm_matmul",
        )(h, g2, w)
    ns = w_small.shape[1]
    return pl.pallas_call(
        _norm_mm_small_kernel, grid=grid,
        in_specs=[x_spec, g_spec, w_spec, pl.BlockSpec((D, ns), lambda i, j: (0, 0))],
        out_specs=[o_spec, pl.BlockSpec((tm, ns), lambda i, j: (i, 0))],
        out_shape=[out_shape, jax.ShapeDtypeStruct((T, ns), F32)],
        scratch_shapes=scratch,
        compiler_params=_cparams("parallel", "arbitrary"), name="norm_matmul_small",
    )(h, g2, w, w_small)


def _out_mlp_kernel(mix_ref, wo_ref, r_ref, g_ref, w1_ref, w2_ref, *rest, glu, final):
    if final:
        gf_ref, o_ref, xn_ref = rest
    else:
        o_ref, xn_ref = rest
    j = pl.program_id(1)

    @pl.when(j == 0)
    def _():
        m = _dot(mix_ref[...], wo_ref[...])
        if glu:
            n = o_ref.shape[1]
            m = m[:, :n] * _sigmoid(m[:, n:])
        h1 = r_ref[...] + m
        o_ref[...] = h1
        xn_ref[...] = _rms(h1, g_ref[...]).astype(BF16)

    a = jnp.maximum(_dot(xn_ref[...], w1_ref[...].astype(BF16)), 0.0)
    o_ref[...] += _dot((a * a).astype(BF16), w2_ref[...].astype(BF16))

    if final:
        @pl.when(j == pl.num_programs(1) - 1)
        def _():
            o_ref[...] = _rms(o_ref[...], gf_ref[...])


def out_proj_mlp(mix, w_out, res, g, w1_stack, w2_stack, layer, g_final=None, *,
                 glu=False, tm=1024, tf=512):
    T, K = mix.shape
    D = res.shape[1]
    FF = w1_stack.shape[2]
    tm = min(tm, T)
    final = g_final is not None
    in_specs = [pl.BlockSpec((tm, K), lambda i, j: (i, 0)),
                pl.BlockSpec(w_out.shape, lambda i, j: (0, 0)),
                pl.BlockSpec((tm, D), lambda i, j: (i, 0)),
                pl.BlockSpec((1, D), lambda i, j: (0, 0)),
                pl.BlockSpec((None, D, tf), lambda i, j: (layer, 0, j)),
                pl.BlockSpec((None, tf, D), lambda i, j: (layer, j, 0))]
    args = [mix, w_out, res, g.reshape(1, D), w1_stack, w2_stack]
    if final:
        in_specs.append(pl.BlockSpec((1, D), lambda i, j: (0, 0)))
        args.append(g_final.reshape(1, D))
    return pl.pallas_call(
        functools.partial(_out_mlp_kernel, glu=glu, final=final), grid=(T // tm, FF // tf),
        in_specs=in_specs,
        out_specs=pl.BlockSpec((tm, D), lambda i, j: (i, 0)),
        out_shape=jax.ShapeDtypeStruct((T, D), F32),
        scratch_shapes=[pltpu.VMEM((tm, D), BF16)],
        compiler_params=_cparams("parallel", "arbitrary"), name="out_proj_mlp",
    )(*args)


def _conv_window(cur, prev, xb_ref, first_chunk):
    xb_ref[0:HALO, :] = jnp.where(first_chunk, jnp.zeros_like(prev), prev)
    xb_ref[HALO:HALO + CHUNK, :] = cur


def _causal_conv(xb_ref, cw_ref):
    i = lax.broadcasted_iota(jnp.int32, (CHUNK, HALO + CHUNK), 0)
    c = lax.broadcasted_iota(jnp.int32, (CHUNK, HALO + CHUNK), 1)
    shift = jnp.concatenate(
        [jnp.where(c == i + (HALO - (CONV_W - 1) + w), 1.0, 0.0) for w in range(CONV_W - 1)],
        axis=0).astype(BF16)
    outs = []
    for c0 in range(0, xb_ref.shape[1], CONV_COLS):
        cols = slice(c0, c0 + CONV_COLS)
        shifted = _dot(shift, xb_ref[:, cols])
        acc = xb_ref[HALO:HALO + CHUNK, cols].astype(F32) * cw_ref[CONV_W - 1:CONV_W, cols]
        for w in range(CONV_W - 1):
            acc = acc + shifted[w * CHUNK:(w + 1) * CHUNK] * cw_ref[w:w + 1, cols]
        outs.append(acc)
    return jnp.concatenate(outs, axis=1)


def _chunk_cumsum(x, causal):
    n = x.shape[1]
    parts = _dot(jnp.where(causal, 1.0, 0.0).astype(BF16), jnp.concatenate(_split3(x), axis=1))
    return parts[:, :n] + (parts[:, n:2 * n] + parts[:, 2 * n:])


def _prev_rows_spec(width):
    blocks_per_chunk = CHUNK // HALO
    return pl.BlockSpec(
        (SEQS_PER_STEP, HALO, width),
        lambda b, c: (b, jnp.maximum(c * blocks_per_chunk - 1, 0), 0))


def _dot_split(a, b):
    return _dot(jnp.concatenate([a[0], a[0], a[1]], axis=1),
                jnp.concatenate([b[0], b[1], b[0]], axis=0))


def _unit_lower_inverses(mats, nilpotent):
    n = mats[0].shape[0]
    row = lax.broadcasted_iota(jnp.int32, (n, n), 0)
    col = lax.broadcasted_iota(jnp.int32, (n, n), 1)
    eye = jnp.where(row == col, 1.0, 0.0)
    ps = [eye - a for a in mats]
    aks = [a.astype(BF16) for a in mats]
    power = 2
    while power < nilpotent:
        aks = [_dot(ak, ak).astype(BF16) for ak in aks]
        ps = [p + _dot(p.astype(BF16), ak) for p, ak in zip(ps, aks)]
        power *= 2
    resid = [eye - _dot_split(_split2(eye + a), _split2(p)) for a, p in zip(mats, ps)]
    return [p + _dot(p.astype(BF16), r.astype(BF16)) for p, r in zip(ps, resid)]


def _gdn_kernel(cur_ref, prev_ref, ab_ref, cw_ref, alog_ref, dtb_ref, ong_ref,
                o_ref, xb_ref, s_ref):
    c = pl.program_id(1)
    nseq = cur_ref.shape[0]
    nqk = GDN_HEADS * GDN_DK
    nconv = 2 * nqk + GDN_HEADS * GDN_DV
    units = [(b, p) for b in range(nseq) for p in range(GDN_HEADS // 2)]
    nu = range(len(units))
    stack = lambda f, p: jnp.concatenate([f(2 * p), f(2 * p + 1)], axis=0)

    @pl.when(c == 0)
    def _():
        s_ref[...] = jnp.zeros_like(s_ref)

    for b in range(nseq):
        _conv_window(cur_ref[b, :, :nconv], prev_ref[b, :, :nconv], xb_ref.at[b], c == 0)
    qkv = [_silu(_causal_conv(xb_ref.at[b], cw_ref)) for b in range(nseq)]

    causal, _ = _causal_masks(CHUNK)
    row = lax.broadcasted_iota(jnp.int32, (2 * CHUNK, 2 * CHUNK), 0)
    lane = lax.broadcasted_iota(jnp.int32, (2 * CHUNK, 2 * CHUNK), 1)
    first = lane < CHUNK
    same_head = (row < CHUNK) == first
    t_row = jnp.where(row < CHUNK, row, row - CHUNK)
    t_col = jnp.where(first, lane, lane - CHUNK)
    causal2 = same_head & (t_row >= t_col)
    strict2 = same_head & (t_row > t_col)

    ab = [ab_ref[b] for b in range(nseq)]
    beta_all = [_sigmoid(t) for t in ab]
    G = [_chunk_cumsum(-jnp.exp(alog_ref[...]) * _softplus(t + dtb_ref[...]), causal) for t in ab]
    G_rows = [jnp.concatenate([t, t], axis=0).T for t in G]
    eG = [jnp.exp(t) for t in G]
    eG_rev = [jnp.exp(t[CHUNK - 1:CHUNK, :] - t) for t in G]

    def heads_of(width, offset, b, p):
        return stack(lambda h: qkv[b][:, offset + h * width:offset + (h + 1) * width], p)

    q = [heads_of(GDN_DK, 0, b, p) for b, p in units]
    k = [heads_of(GDN_DK, nqk, b, p) for b, p in units]
    v = [heads_of(GDN_DV, 2 * nqk, b, p) for b, p in units]
    sq_hi, sq_lo = _split2(jnp.concatenate([t * t for t in q + k], axis=0))
    ones = jnp.ones((GDN_DK, GDN_DK), BF16)
    inv_norm = lax.rsqrt(_dot(sq_hi, ones) + _dot(sq_lo, ones) + L2_EPS)
    rows = 2 * CHUNK
    q = [q[u] * inv_norm[u * rows:(u + 1) * rows] * (GDN_DK ** -0.5) for u in nu]
    k = [k[u] * inv_norm[(len(units) + u) * rows:(len(units) + u + 1) * rows] for u in nu]
    col_of = lambda a, b, p, off=0: stack(lambda h: a[b][:, off + h:off + h + 1], p)
    beta = [col_of(beta_all, b, p, GDN_HEADS) for b, p in units]
    eg = [col_of(eG, b, p) for b, p in units]
    decay = []
    for b, p in units:
        g_row = jnp.where(first[0:1], G_rows[b][2 * p:2 * p + 1, :], G_rows[b][2 * p + 1:2 * p + 2, :])
        diff = col_of(G, b, p) - g_row
        decay.append(jnp.where(causal2, jnp.exp(jnp.where(causal2, diff, 0.0)), 0.0))
    kb = [t.astype(BF16) for t in k]
    qkk = [_dot_nt(jnp.concatenate([q[u].astype(BF16), kb[u]], axis=0), kb[u]) for u in nu]
    t_inv = _unit_lower_inverses(
        [jnp.where(strict2, beta[u] * qkk[u][rows:] * decay[u], 0.0) for u in nu], CHUNK)
    uw = [_dot(t_inv[u].astype(BF16),
               jnp.concatenate([v[u] * beta[u], k[u] * (beta[u] * eg[u])], axis=1).astype(BF16))
          for u in nu]
    qe = [q[u] * eg[u] for u in nu]
    halves = (slice(0, CHUNK), slice(CHUNK, rows))
    S = [[s_ref[b, 2 * p + i] for i in range(2)] for b, p in units]
    Sb = [[t.astype(BF16) for t in pair] for pair in S]
    ws = [[_dot(jnp.concatenate([uw[u][halves[i], GDN_DV:], qe[u][halves[i]]], axis=0).astype(BF16),
                Sb[u][i]) for i in range(2)] for u in nu]
    vb = [(uw[u][:, :GDN_DV] - jnp.concatenate([ws[u][0][:CHUNK], ws[u][1][:CHUNK]], axis=0))
          .astype(BF16) for u in nu]
    o = [jnp.concatenate([ws[u][0][CHUNK:], ws[u][1][CHUNK:]], axis=0)
         + _dot((qkk[u][:rows] * decay[u]).astype(BF16), vb[u]) for u in nu]
    k_dec = [(k[u] * col_of(eG_rev, b, p)).astype(BF16) for u, (b, p) in enumerate(units)]
    s_new = [[eG[b][CHUNK - 1:CHUNK, 2 * p + i:2 * p + i + 1] * S[u][i]
              + _dot_tn(k_dec[u][halves[i]], vb[u][halves[i]]) for i in range(2)]
             for u, (b, p) in enumerate(units)]
    for u, (b, p) in enumerate(units):
        for i in range(2):
            s_ref[b, 2 * p + i] = s_new[u][i]
    for u, (b, p) in enumerate(units):
        for i in range(2):
            h = 2 * p + i
            gate = cur_ref[b, :, nconv + h * GDN_DV:nconv + (h + 1) * GDN_DV].astype(F32)
            o_ref[b, :, h * GDN_DV:(h + 1) * GDN_DV] = (
                _rms(o[u][halves[i]], ong_ref[...]) * _silu(gate)).astype(o_ref.dtype)


def gdn_core(proj, ab, conv_w, a_log, dt_bias, o_norm_g):
    batch, seq_len, W = proj.shape
    nv = GDN_HEADS * GDN_DV
    nconv = conv_w.shape[1]
    nb = SEQS_PER_STEP
    pad = lambda a: jnp.pad(a.reshape(1, -1), ((0, 0), (0, LANES - a.shape[-1])))
    const = lambda shape: pl.BlockSpec(shape, lambda b, c: (0, 0))
    return pl.pallas_call(
        _gdn_kernel, grid=(batch // nb, seq_len // CHUNK),
        in_specs=[pl.BlockSpec((nb, CHUNK, W), lambda b, c: (b, c, 0)),
                  _prev_rows_spec(W),
                  pl.BlockSpec((nb, CHUNK, LANES), lambda b, c: (b, c, 0)),
                  const((CONV_W, nconv)), const((1, LANES)), const((1, LANES)),
                  const((1, GDN_DV))],
        out_specs=pl.BlockSpec((nb, CHUNK, nv), lambda b, c: (b, c, 0)),
        out_shape=jax.ShapeDtypeStruct((batch, seq_len, nv), BF16),
        scratch_shapes=[pltpu.VMEM((nb, HALO + CHUNK, nconv), BF16),
                        pltpu.VMEM((nb, GDN_HEADS, GDN_DK, GDN_DV), F32)],
        compiler_params=_cparams("parallel", "arbitrary"), name="gdn_core",
    )(proj, proj, ab, conv_w, pad(a_log), pad(dt_bias), o_norm_g.reshape(1, GDN_DV))


def gdn_mixer(h, g_mix, w_in_stack, layer, conv_w, a_log, dt_bias, o_norm_g, *, batch, seq_len):
    nmain = conv_w.shape[1] + GDN_HEADS * GDN_DV
    w_small = jnp.pad(w_in_stack[layer][:, nmain:],
                      ((0, 0), (0, LANES - 2 * GDN_HEADS))).astype(BF16)
    proj, ab = norm_matmul(h, g_mix, w_in_stack, layer, nmain, w_small, out_dtype=BF16)
    o = gdn_core(proj.reshape(batch, seq_len, -1), ab.reshape(batch, seq_len, -1),
                 conv_w, a_log, dt_bias, o_norm_g)
    return o.reshape(batch * seq_len, -1)


def _m2_kernel(cur_ref, prev_ref, dt_ref, cw_ref, cb_ref, dtb_ref, alog_ref, dsk_ref,
               ng_ref, sel_ref, o_ref, xb_ref, s_ref):
    c = pl.program_id(1)
    nseq = cur_ref.shape[0]
    seqs = range(nseq)
    inner = dsk_ref.shape[1]
    nbc = M2_GROUPS * M2_STATE
    pair_w = 2 * M2_HEAD_DIM
    pairs_per_group = inner // M2_GROUPS // pair_w
    gw = inner // M2_GROUPS

    @pl.when(c == 0)
    def _():
        s_ref[...] = jnp.zeros_like(s_ref)

    for b in seqs:
        _conv_window(cur_ref[b, :, inner:], prev_ref[b, :, inner:], xb_ref.at[b], c == 0)
    xbc = [_silu(_causal_conv(xb_ref.at[b], cw_ref) + cb_ref[...]) for b in seqs]
    x = [t[:, :inner] for t in xbc]

    causal, _ = _causal_masks(CHUNK)
    dt = [_softplus(dt_ref[b] + dtb_ref[...]) for b in seqs]
    cum = [_chunk_cumsum(t * (-jnp.exp(alog_ref[...])), causal) for t in dt]
    cum_rows = [jnp.concatenate([t, t], axis=0).T for t in cum]
    per_head = jnp.concatenate(
        [a for b in seqs
         for a in (dt[b], cum[b], jnp.exp(cum[b]), jnp.exp(cum[b][CHUNK - 1:CHUNK, :] - cum[b]))],
        axis=0)
    rows = per_head.shape[0]
    parts = _dot(jnp.concatenate(_split2(per_head), axis=0), sel_ref[...])
    spread = parts[:rows] + parts[rows:]
    dt_x = [spread[(4 * b) * CHUNK:(4 * b + 1) * CHUNK] for b in seqs]
    cum_x = [spread[(4 * b + 1) * CHUNK:(4 * b + 2) * CHUNK] for b in seqs]
    ecum = [spread[(4 * b + 2) * CHUNK:(4 * b + 3) * CHUNK] for b in seqs]
    xdt = [x[b] * dt_x[b] for b in seqs]
    xd = [xdt[b] * spread[(4 * b + 3) * CHUNK:(4 * b + 4) * CHUNK] for b in seqs]

    lane = lax.broadcasted_iota(jnp.int32, (CHUNK, pair_w), 1)
    row = lax.broadcasted_iota(jnp.int32, (CHUNK, pair_w), 0)
    first = lane < M2_HEAD_DIM
    causal2 = row >= jnp.where(first, lane, lane - M2_HEAD_DIM)

    units = [(b, g) for b in seqs for g in range(M2_GROUPS)]
    gcol = lambda g: slice(g * gw, (g + 1) * gw)
    Bg = [xbc[b][:, inner + g * M2_STATE:inner + (g + 1) * M2_STATE].astype(BF16) for b, g in units]
    Cg = [xbc[b][:, inner + nbc + g * M2_STATE:inner + nbc + (g + 1) * M2_STATE].astype(BF16)
          for b, g in units]
    cb2 = [_dot_nt(Cg[u], jnp.concatenate([Bg[u], Bg[u]], axis=0))
           for u in range(len(units))]
    S = [s_ref[b, g] for b, g in units]
    y_off = [_dot(Cg[u], S[u].astype(BF16)) * ecum[b][:, gcol(g)] for u, (b, g) in enumerate(units)]
    s_new = [ecum[b][CHUNK - 1:CHUNK, gcol(g)] * S[u]
             + _dot_tn(Bg[u], xd[b][:, gcol(g)].astype(BF16)) for u, (b, g) in enumerate(units)]
    for u, (b, g) in enumerate(units):
        s_ref[b, g] = s_new[u]
    y_diag = []
    for u, (b, g) in enumerate(units):
        for m in range(pairs_per_group):
            p = g * pairs_per_group + m
            cols = slice(p * pair_w, (p + 1) * pair_w)
            h1 = 2 * p
            row_pair = jnp.where(first[0:1], cum_rows[b][h1:h1 + 1, :], cum_rows[b][h1 + 1:h1 + 2, :])
            diff = cum_x[b][:, cols] - row_pair
            lmat = jnp.where(causal2, jnp.exp(jnp.where(causal2, diff, 0.0)), 0.0)
            xp = xdt[b][:, cols]
            x_bd = jnp.concatenate([jnp.where(first, xp, 0.0), jnp.where(first, 0.0, xp)], axis=0)
            y_diag.append(_dot((cb2[u] * lmat).astype(BF16), x_bd.astype(BF16)))
    for u, (b, g) in enumerate(units):
        cols = gcol(g)
        yd = jnp.concatenate(y_diag[u * pairs_per_group:(u + 1) * pairs_per_group], axis=1)
        y = ((yd + y_off[u] + dsk_ref[:, cols] * x[b][:, cols])
             * _silu(cur_ref[b, :, cols].astype(F32)))
        o_ref[b, :, cols] = _rms(y, ng_ref[:, cols]).astype(o_ref.dtype)


def m2_core(proj, dt_raw, conv_w, conv_b, dt_bias, a_log, d_skip, norm_g):
    batch, seq_len, W = proj.shape
    nconv = conv_w.shape[1]
    inner = W - nconv
    nb = SEQS_PER_STEP
    pad = lambda a: jnp.pad(a.reshape(1, -1), ((0, 0), (0, LANES - a.shape[-1])))
    head_of_lane = jnp.arange(inner, dtype=jnp.int32) // M2_HEAD_DIM
    sel = (jnp.arange(LANES, dtype=jnp.int32)[:, None] == head_of_lane[None, :]).astype(BF16)
    d_x = jnp.repeat(d_skip, M2_HEAD_DIM).reshape(1, inner)
    const = lambda shape: pl.BlockSpec(shape, lambda b, c: (0, 0))
    return pl.pallas_call(
        _m2_kernel, grid=(batch // nb, seq_len // CHUNK),
        in_specs=[pl.BlockSpec((nb, CHUNK, W), lambda b, c: (b, c, 0)),
                  _prev_rows_spec(W),
                  pl.BlockSpec((nb, CHUNK, LANES), lambda b, c: (b, c, 0)),
                  const((CONV_W, nconv)), const((1, nconv)), const((1, LANES)),
                  const((1, LANES)), const((1, inner)), const((1, inner)),
                  const((LANES, inner))],
        out_specs=pl.BlockSpec((nb, CHUNK, inner), lambda b, c: (b, c, 0)),
        out_shape=jax.ShapeDtypeStruct((batch, seq_len, inner), BF16),
        scratch_shapes=[pltpu.VMEM((nb, HALO + CHUNK, nconv), BF16),
                        pltpu.VMEM((nb, M2_GROUPS, M2_STATE, inner // M2_GROUPS), F32)],
        compiler_params=_cparams("parallel", "arbitrary"), name="m2_core",
    )(proj, proj, dt_raw, conv_w, conv_b.reshape(1, nconv), pad(dt_bias), pad(a_log),
      d_x, norm_g.reshape(1, inner), sel)


def m2_mixer(h, g_mix, w_in_stack, layer, conv_w, conv_b, dt_bias, a_log, d_skip, norm_g,
             *, batch, seq_len):
    heads = dt_bias.shape[0]
    nmain = w_in_stack.shape[2] - heads
    w_small = jnp.pad(w_in_stack[layer][:, nmain:], ((0, 0), (0, LANES - heads))).astype(BF16)
    proj, dt_raw = norm_matmul(h, g_mix, w_in_stack, layer, nmain, w_small, out_dtype=BF16)
    y = m2_core(proj.reshape(batch, seq_len, -1), dt_raw.reshape(batch, seq_len, -1),
                conv_w, conv_b, dt_bias, a_log, d_skip, norm_g)
    return y.reshape(batch * seq_len, -1)


def _s5_prep_kernel(lr_ref, li_ref, ldt_ref, br_ref, bi_ref, lbr_ref, lbi_ref, bbr_ref, bbi_ref):
    lr, li = lr_ref[...], li_ref[...]
    dt = jnp.exp(ldt_ref[...])
    mag = jnp.exp(lr * dt)
    lbr = mag * jnp.cos(li * dt)
    lbi = mag * jnp.sin(li * dt)
    nr = lbr - 1.0
    den = lr * lr + li * li
    qr = (nr * lr + lbi * li) / den
    qi = (lbi * lr - nr * li) / den
    br, bi = br_ref[...], bi_ref[...]
    lbr_ref[...] = lbr
    lbi_ref[...] = lbi
    bbr_ref[...] = qr * br - qi * bi
    bbi_ref[...] = qr * bi + qi * br


def _s5_kernel(u_ref, b_ref, cre_ref, cim_ref, lam_ref, d_ref, y_ref, bu_ref, st_ref, *, steps):
    nslab = b_ref.shape[0]
    half = b_ref.shape[2] // 2

    @pl.when(pl.program_id(0) == 0)
    def _():
        st_ref[...] = jnp.zeros_like(st_ref)

    batch, _, d_model = u_ref.shape
    rows = batch * steps

    dst = lax.broadcasted_iota(jnp.int32, (rows, rows), 0)
    src = lax.broadcasted_iota(jnp.int32, (rows, rows), 1)
    perm = jnp.where((dst // batch == src % steps) & (dst % batch == src // steps),
                     1.0, 0.0).astype(BF16)
    u_hi, u_lo = _split2(u_ref[...].reshape(rows, d_model))
    u_hi = _dot(perm, u_hi)
    u_tm = u_hi + _dot(perm, u_lo)
    ub = u_hi.astype(BF16)

    for s in range(nslab):
        bu_ref[:, 2 * half * s:2 * half * (s + 1)] = _dot(
            ub[:, s * S5_SLAB:(s + 1) * S5_SLAB], b_ref[s])

    for s in range(nslab):
        re = slice(2 * half * s, 2 * half * s + half)
        im = slice(2 * half * s + half, 2 * half * (s + 1))
        lam_r = jnp.broadcast_to(lam_ref[:, re], (batch, half))
        lam_i = jnp.broadcast_to(lam_ref[:, im], (batch, half))

        def step(t, carry, re=re, im=im, lam_r=lam_r, lam_i=lam_i):
            xr, xi = carry
            at_t = pl.ds(pl.multiple_of(t * batch, batch), batch)
            nr = lam_r * xr - lam_i * xi + bu_ref[at_t, re]
            ni = lam_r * xi + lam_i * xr + bu_ref[at_t, im]
            bu_ref[at_t, re] = nr
            bu_ref[at_t, im] = ni
            return nr, ni

        xr, xi = lax.fori_loop(0, steps, step, (st_ref[:, re], st_ref[:, im]))
        st_ref[:, re] = xr
        st_ref[:, im] = xi

    ys = []
    for s in range(nslab):
        re = slice(2 * half * s, 2 * half * s + half)
        im = slice(2 * half * s + half, 2 * half * (s + 1))
        cols = slice(s * S5_SLAB, (s + 1) * S5_SLAB)
        y = (_dot(bu_ref[:, re].astype(BF16), cre_ref[s])
             - _dot(bu_ref[:, im].astype(BF16), cim_ref[s])
             + d_ref[:, cols] * u_tm[:, cols])
        ys.append(jax.nn.gelu(y, approximate=True).astype(BF16))
    y_sm = _dot_tn(perm, jnp.concatenate(ys, axis=1))
    y_ref[...] = y_sm.astype(y_ref.dtype).reshape(batch, steps, d_model)


def s5_core(u, lam_re, lam_im, log_dt, b_re, b_im, c_re, c_im, d_skip, *, steps=32):
    batch, L, D = u.shape
    assert batch == SUBLANES, "one time step's batch entries fill one sublane tile"
    steps = min(steps, L)
    G, P = lam_re.shape
    K = S5_GROUP
    nslab = D // S5_SLAB
    gps = S5_SLAB // K
    half = gps * P

    rep = lambda a: jnp.repeat(a, K, axis=-1)
    flat = jax.ShapeDtypeStruct((G, P * K), F32)
    lbr, lbi, bbr, bbi = pl.pallas_call(
        _s5_prep_kernel, out_shape=[flat] * 4, name="s5_prep",
    )(rep(lam_re), rep(lam_im), jnp.broadcast_to(log_dt[:, None], (G, P * K)),
      b_re.reshape(G, P * K), b_im.reshape(G, P * K))

    eye = jnp.eye(gps, dtype=F32)
    to_slab = lambda a: a.reshape(G, P, K).transpose(0, 2, 1).reshape(nslab, gps, K, P)
    b_blk = jnp.concatenate(
        [jnp.einsum('sjkp,jJ->sjkJp', to_slab(a), eye).reshape(nslab, S5_SLAB, half)
         for a in (bbr, bbi)], axis=-1).astype(BF16)
    c_blk = [jnp.einsum('sjkp,jJ->sJpjk', a.reshape(nslab, gps, K, P), eye)
             .reshape(nslab, half, S5_SLAB).astype(BF16) for a in (c_re, c_im)]
    lam = jnp.stack([lbr[:, ::K].reshape(nslab, half), lbi[:, ::K].reshape(nslab, half)],
                    axis=1).reshape(1, nslab * 2 * half)

    const3 = lambda shape: pl.BlockSpec(shape, lambda i: (0, 0, 0))
    return pl.pallas_call(
        functools.partial(_s5_kernel, steps=steps), grid=(L // steps,),
        in_specs=[pl.BlockSpec((batch, steps, D), lambda i: (0, i, 0)),
                  const3((nslab, S5_SLAB, 2 * half)),
                  const3((nslab, half, S5_SLAB)), const3((nslab, half, S5_SLAB)),
                  pl.BlockSpec((1, nslab * 2 * half), lambda i: (0, 0)),
                  pl.BlockSpec((1, D), lambda i: (0, 0))],
        out_specs=pl.BlockSpec((batch, steps, D), lambda i: (0, i, 0)),
        out_shape=jax.ShapeDtypeStruct((batch, L, D), BF16),
        scratch_shapes=[pltpu.VMEM((batch * steps, nslab * 2 * half), F32),
                        pltpu.VMEM((batch, nslab * 2 * half), F32)],
        compiler_params=_cparams("arbitrary"), name="s5_core",
    )(u, b_blk, c_blk[0], c_blk[1], lam, d_skip.reshape(1, D))


def s5_mixer(h, g_mix, w_in_stack, layer, lam_re, lam_im, log_dt, b_re, b_im, c_re, c_im, d_skip,
             *, batch, seq_len):
    T, D = h.shape
    u = norm_matmul(h, g_mix, w_in_stack, layer, D, out_dtype=F32)
    y = s5_core(u.reshape(batch, seq_len, D), lam_re, lam_im, log_dt, b_re, b_im, c_re, c_im,
                d_skip)
    return y.reshape(T, D)


def kernel(x, norm_mix_g, norm_mlp_g, mlp_w1, mlp_w2, gdn_w_in, gdn_conv_w, gdn_a_log, gdn_dt_bias, gdn_o_norm_g, gdn_w_out, s5_w_in, s5_lam_re, s5_lam_im, s5_log_dt, s5_b_re, s5_b_im, s5_c_re, s5_c_im, s5_d, s5_w_out, m2_w_in, m2_conv_w, m2_conv_b, m2_dt_bias, m2_a_log, m2_d, m2_norm_g, m2_w_out, final_norm_g):
    batch, seq_len, d_model = x.shape
    depth = norm_mix_g.shape[0]
    h = x.reshape(batch * seq_len, d_model)
    for i in range(depth):
        kind, j = i % 3, i // 3
        if kind == 0:
            mix = gdn_mixer(h, norm_mix_g[i], gdn_w_in, j, gdn_conv_w[j], gdn_a_log[j],
                            gdn_dt_bias[j], gdn_o_norm_g[j], batch=batch, seq_len=seq_len)
            w_out = gdn_w_out[j]
        elif kind == 1:
            mix = s5_mixer(h, norm_mix_g[i], s5_w_in, j, s5_lam_re[j], s5_lam_im[j], s5_log_dt[j],
                           s5_b_re[j], s5_b_im[j], s5_c_re[j], s5_c_im[j], s5_d[j],
                           batch=batch, seq_len=seq_len)
            w_out = s5_w_out[j]
        else:
            mix = m2_mixer(h, norm_mix_g[i], m2_w_in, j, m2_conv_w[j], m2_conv_b[j], m2_dt_bias[j],
                           m2_a_log[j], m2_d[j], m2_norm_g[j], batch=batch, seq_len=seq_len)
            w_out = m2_w_out[j]
        h = out_proj_mlp(mix, w_out.astype(BF16), h, norm_mlp_g[i], mlp_w1, mlp_w2, i,
                         final_norm_g if i == depth - 1 else None, glu=(kind == 1))
    return h.reshape(batch, seq_len, d_model)
```

```python
import functools

import jax
import jax.numpy as jnp
from jax import lax
from jax.experimental import pallas as pl
from jax.experimental.pallas import tpu as pltpu

F32 = jnp.float32
BF16 = jnp.bfloat16

RMS_EPS = 1e-6
L2_EPS = 1e-6
CHUNK = 64
CONV_W = 4
LANES = 128
SUBLANES = 8
HALO = 2 * SUBLANES
CONV_COLS = 512
SEQS_PER_STEP = 4
VMEM_LIMIT = 56 * 1024 * 1024

GDN_HEADS = 8
GDN_DK = 128
GDN_DV = 128
S5_GROUP = 16
S5_STATE = 64
S5_SLAB = 128
M2_HEAD_DIM = 64
M2_STATE = 128
M2_GROUPS = 8


def _cparams(*sem):
    return pltpu.CompilerParams(dimension_semantics=sem, vmem_limit_bytes=VMEM_LIMIT)


def _rms(x, g):
    return x * lax.rsqrt(jnp.mean(x * x, axis=-1, keepdims=True) + RMS_EPS) * g


def _sigmoid(x):
    return 0.5 * (jnp.tanh(0.5 * x) + 1.0)


def _silu(x):
    return x * _sigmoid(x)


def _softplus(x):
    return jnp.maximum(x, 0.0) + jnp.log(1.0 + jnp.exp(-jnp.abs(x)))


def _dot(a, b):
    return jnp.dot(a, b, preferred_element_type=F32)


def _split2(x):
    hi = x.astype(BF16)
    return hi, (x - hi.astype(F32)).astype(BF16)


def _split3(x):
    hi = x.astype(BF16)
    r = x - hi.astype(F32)
    mid = r.astype(BF16)
    return hi, mid, (r - mid.astype(F32)).astype(BF16)


def _dot_nt(a, b):
    return lax.dot_general(a, b, (((1,), (1,)), ((), ())), preferred_element_type=F32)


def _dot_tn(a, b):
    return lax.dot_general(a, b, (((0,), (0,)), ((), ())), preferred_element_type=F32)


def _causal_masks(n):
    row = lax.broadcasted_iota(jnp.int32, (n, n), 0)
    col = lax.broadcasted_iota(jnp.int32, (n, n), 1)
    return row >= col, row > col


def _norm_mm_kernel(x_ref, g_ref, w_ref, *rest, small):
    if small:
        ws_ref, o_ref, os_ref, xn_ref = rest
    else:
        o_ref, xn_ref = rest
    i, j = pl.program_id(0), pl.program_id(1)
    last = pl.num_programs(1) - 1
    slot = i % 2

    @pl.when((i == 0) & (j == 0))
    def _():
        xn_ref[0] = _rms(x_ref[...], g_ref[...]).astype(BF16)

    def project():
        o_ref[...] = _dot(xn_ref[slot], w_ref[...]).astype(o_ref.dtype)

    if small:
        @pl.when(j == 0)
        def _():
            os_ref[...] = _dot(xn_ref[slot], ws_ref[...])

    pl.when(j < last)(project)

    @pl.when(j == last)
    def _():
        project()
        xn_ref[1 - slot] = _rms(x_ref[...], g_ref[...]).astype(BF16)


def norm_matmul(h, g, w_stack, layer, n_out, w_small=None, *, out_dtype, tm=1024, tn=1024):
    T, D = h.shape
    N = n_out
    tm = min(tm, T)
    tn = min(tn, N // 2)
    ni, nj = T // tm, N // tn
    x_spec = pl.BlockSpec(
        (tm, D), lambda i, j: (jnp.where(j == nj - 1, jnp.minimum(i + 1, ni - 1), i), 0))
    g_spec = pl.BlockSpec((1, D), lambda i, j: (0, 0))
    w_spec = pl.BlockSpec((None, D, tn), lambda i, j: (layer, 0, j))
    out_shape = jax.ShapeDtypeStruct((T, N), out_dtype)
    o_spec = pl.BlockSpec((tm, tn), lambda i, j: (i, j))
    in_specs = [x_spec, g_spec, w_spec]
    args = [h, g.reshape(1, D), w_stack]
    small = w_small is not None
    if small:
        ns = w_small.shape[1]
        in_specs.append(pl.BlockSpec((D, ns), lambda i, j: (0, 0)))
        args.append(w_small)
        o_spec = [o_spec, pl.BlockSpec((tm, ns), lambda i, j: (i, 0))]
        out_shape = [out_shape, jax.ShapeDtypeStruct((T, ns), F32)]
    return pl.pallas_call(
        functools.partial(_norm_mm_kernel, small=small), grid=(ni, nj), in_specs=in_specs,
        out_specs=o_spec, out_shape=out_shape,
        scratch_shapes=[pltpu.VMEM((2, tm, D), BF16)],
        compiler_params=_cparams("arbitrary", "arbitrary"), name="norm_matmul",
    )(*args)


def _out_mlp_kernel(mix_ref, wo_ref, r_ref, g_ref, w1_ref, w2_ref, *rest, glu, final):
    if final:
        gf_ref, o_ref, xn_ref = rest
    else:
        o_ref, xn_ref = rest
    j = pl.program_id(1)

    @pl.when(j == 0)
    def _():
        m = _dot(mix_ref[...], wo_ref[...])
        if glu:
            n = o_ref.shape[1]
            m = m[:, :n] * _sigmoid(m[:, n:])
        h1 = r_ref[...] + m
        o_ref[...] = h1
        xn_ref[...] = _rms(h1, g_ref[...]).astype(BF16)

    a = jnp.maximum(_dot(xn_ref[...], w1_ref[...]), 0.0)
    o_ref[...] += _dot((a * a).astype(BF16), w2_ref[...])

    if final:
        @pl.when(j == pl.num_programs(1) - 1)
        def _():
            o_ref[...] = _rms(o_ref[...], gf_ref[...])


def out_proj_mlp(mix, w_out, res, g, w1_stack, w2_stack, layer, g_final=None, *,
                 glu=False, tm=1024, tf=1024):
    T, K = mix.shape
    D = res.shape[1]
    FF = w1_stack.shape[2]
    tm = min(tm, T)
    final = g_final is not None
    in_specs = [pl.BlockSpec((tm, K), lambda i, j: (i, 0)),
                pl.BlockSpec(w_out.shape, lambda i, j: (0, 0)),
                pl.BlockSpec((tm, D), lambda i, j: (i, 0)),
                pl.BlockSpec((1, D), lambda i, j: (0, 0)),
                pl.BlockSpec((None, D, tf), lambda i, j: (layer, 0, j)),
                pl.BlockSpec((None, tf, D), lambda i, j: (layer, j, 0))]
    args = [mix, w_out, res, g.reshape(1, D), w1_stack, w2_stack]
    if final:
        in_specs.append(pl.BlockSpec((1, D), lambda i, j: (0, 0)))
        args.append(g_final.reshape(1, D))
    return pl.pallas_call(
        functools.partial(_out_mlp_kernel, glu=glu, final=final), grid=(T // tm, FF // tf),
        in_specs=in_specs,
        out_specs=pl.BlockSpec((tm, D), lambda i, j: (i, 0)),
        out_shape=jax.ShapeDtypeStruct((T, D), F32),
        scratch_shapes=[pltpu.VMEM((tm, D), BF16)],
        compiler_params=_cparams("parallel", "arbitrary"), name="out_proj_mlp",
    )(*args)


def _conv_window(cur, prev, xb_ref, first_chunk):
    xb_ref[0:HALO, :] = jnp.where(first_chunk, jnp.zeros_like(prev), prev)
    xb_ref[HALO:HALO + CHUNK, :] = cur


def _causal_conv(xb_ref, cw_ref):
    i = lax.broadcasted_iota(jnp.int32, (CHUNK, HALO + CHUNK), 0)
    c = lax.broadcasted_iota(jnp.int32, (CHUNK, HALO + CHUNK), 1)
    shift = jnp.concatenate(
        [jnp.where(c == i + (HALO - (CONV_W - 1) + w), 1.0, 0.0) for w in range(CONV_W - 1)],
        axis=0).astype(BF16)
    outs = []
    for c0 in range(0, xb_ref.shape[1], CONV_COLS):
        cols = slice(c0, c0 + CONV_COLS)
        shifted = _dot(shift, xb_ref[:, cols])
        acc = xb_ref[HALO:HALO + CHUNK, cols].astype(F32) * cw_ref[CONV_W - 1:CONV_W, cols]
        for w in range(CONV_W - 1):
            acc = acc + shifted[w * CHUNK:(w + 1) * CHUNK] * cw_ref[w:w + 1, cols]
        outs.append(acc)
    return jnp.concatenate(outs, axis=1)


def _chunk_cumsum(x, causal):
    n = x.shape[1]
    parts = _dot(jnp.where(causal, 1.0, 0.0).astype(BF16), jnp.concatenate(_split3(x), axis=1))
    return parts[:, :n] + (parts[:, n:2 * n] + parts[:, 2 * n:])


def _prev_rows_spec(width):
    blocks_per_chunk = CHUNK // HALO
    return pl.BlockSpec(
        (SEQS_PER_STEP, HALO, width),
        lambda b, c: (b, jnp.maximum(c * blocks_per_chunk - 1, 0), 0))


def _dot_split(a, b):
    return _dot(jnp.concatenate([a[0], a[0], a[1]], axis=1),
                jnp.concatenate([b[0], b[1], b[0]], axis=0))


def _unit_lower_inverses(mats, nilpotent):
    n = mats[0].shape[0]
    row = lax.broadcasted_iota(jnp.int32, (n, n), 0)
    col = lax.broadcasted_iota(jnp.int32, (n, n), 1)
    eye = jnp.where(row == col, 1.0, 0.0)
    ps = [eye - a for a in mats]
    aks = [a.astype(BF16) for a in mats]
    power = 2
    while power < nilpotent:
        aks = [_dot(ak, ak).astype(BF16) for ak in aks]
        ps = [p + _dot(p.astype(BF16), ak) for p, ak in zip(ps, aks)]
        power *= 2
    resid = [eye - _dot_split(_split2(eye + a), _split2(p)) for a, p in zip(mats, ps)]
    return [p + _dot(p.astype(BF16), r.astype(BF16)) for p, r in zip(ps, resid)]


def _gdn_kernel(cur_ref, prev_ref, ab_ref, cw_ref, alog_ref, dtb_ref, ong_ref,
                o_ref, xb_ref, s_ref):
    c = pl.program_id(1)
    nseq = cur_ref.shape[0]
    nqk = GDN_HEADS * GDN_DK
    nconv = 2 * nqk + GDN_HEADS * GDN_DV
    units = [(b, p) for b in range(nseq) for p in range(GDN_HEADS // 2)]
    nu = range(len(units))
    stack = lambda f, p: jnp.concatenate([f(2 * p), f(2 * p + 1)], axis=0)

    @pl.when(c == 0)
    def _():
        s_ref[...] = jnp.zeros_like(s_ref)

    for b in range(nseq):
        _conv_window(cur_ref[b, :, :nconv], prev_ref[b, :, :nconv], xb_ref.at[b], c == 0)
    qkv = [_silu(_causal_conv(xb_ref.at[b], cw_ref)) for b in range(nseq)]

    causal, _ = _causal_masks(CHUNK)
    row = lax.broadcasted_iota(jnp.int32, (2 * CHUNK, 2 * CHUNK), 0)
    lane = lax.broadcasted_iota(jnp.int32, (2 * CHUNK, 2 * CHUNK), 1)
    first = lane < CHUNK
    same_head = (row < CHUNK) == first
    t_row = jnp.where(row < CHUNK, row, row - CHUNK)
    t_col = jnp.where(first, lane, lane - CHUNK)
    causal2 = same_head & (t_row >= t_col)
    strict2 = same_head & (t_row > t_col)

    ab = [ab_ref[b] for b in range(nseq)]
    beta_all = [_sigmoid(t) for t in ab]
    G = [_chunk_cumsum(-jnp.exp(alog_ref[...]) * _softplus(t + dtb_ref[...]), causal) for t in ab]
    G_rows = [jnp.concatenate([t, t], axis=0).T for t in G]
    eG = [jnp.exp(t) for t in G]
    eG_rev = [jnp.exp(t[CHUNK - 1:CHUNK, :] - t) for t in G]

    def heads_of(width, offset, b, p):
        return stack(lambda h: qkv[b][:, offset + h * width:offset + (h + 1) * width], p)

    q = [heads_of(GDN_DK, 0, b, p) for b, p in units]
    k = [heads_of(GDN_DK, nqk, b, p) for b, p in units]
    v = [heads_of(GDN_DV, 2 * nqk, b, p) for b, p in units]
    squares = jnp.concatenate([t * t for t in q + k], axis=0).astype(BF16)
    inv_norm = lax.rsqrt(_dot(squares, jnp.ones((GDN_DK, GDN_DK), BF16)) + L2_EPS)
    rows = 2 * CHUNK
    q = [q[u] * inv_norm[u * rows:(u + 1) * rows] * (GDN_DK ** -0.5) for u in nu]
    k = [k[u] * inv_norm[(len(units) + u) * rows:(len(units) + u + 1) * rows] for u in nu]
    col_of = lambda a, b, p, off=0: stack(lambda h: a[b][:, off + h:off + h + 1], p)
    beta = [col_of(beta_all, b, p, GDN_HEADS) for b, p in units]
    eg = [col_of(eG, b, p) for b, p in units]
    decay = []
    for b, p in units:
        g_row = jnp.where(first[0:1], G_rows[b][2 * p:2 * p + 1, :], G_rows[b][2 * p + 1:2 * p + 2, :])
        diff = col_of(G, b, p) - g_row
        decay.append(jnp.where(causal2, jnp.exp(jnp.where(causal2, diff, 0.0)), 0.0))
    kb = [t.astype(BF16) for t in k]
    qkk = [_dot_nt(jnp.concatenate([q[u].astype(BF16), kb[u]], axis=0), kb[u]) for u in nu]
    t_inv = _unit_lower_inverses(
        [jnp.where(strict2, beta[u] * qkk[u][rows:] * decay[u], 0.0) for u in nu], CHUNK)
    uw = [_dot(t_inv[u].astype(BF16),
               jnp.concatenate([v[u] * beta[u], k[u] * (beta[u] * eg[u])], axis=1).astype(BF16))
          for u in nu]
    qe = [q[u] * eg[u] for u in nu]
    halves = (slice(0, CHUNK), slice(CHUNK, rows))
    S = [[s_ref[b, 2 * p + i] for i in range(2)] for b, p in units]
    Sb = [[t.astype(BF16) for t in pair] for pair in S]
    ws = [[_dot(jnp.concatenate([uw[u][halves[i], GDN_DV:], qe[u][halves[i]]], axis=0).astype(BF16),
                Sb[u][i]) for i in range(2)] for u in nu]
    vb = [(uw[u][:, :GDN_DV] - jnp.concatenate([ws[u][0][:CHUNK], ws[u][1][:CHUNK]], axis=0))
          .astype(BF16) for u in nu]
    o = [jnp.concatenate([ws[u][0][CHUNK:], ws[u][1][CHUNK:]], axis=0)
         + _dot((qkk[u][:rows] * decay[u]).astype(BF16), vb[u]) for u in nu]
    k_dec = [(k[u] * col_of(eG_rev, b, p)).astype(BF16) for u, (b, p) in enumerate(units)]
    s_new = [[eG[b][CHUNK - 1:CHUNK, 2 * p + i:2 * p + i + 1] * S[u][i]
              + _dot_tn(k_dec[u][halves[i]], vb[u][halves[i]]) for i in range(2)]
             for u, (b, p) in enumerate(units)]
    for u, (b, p) in enumerate(units):
        for i in range(2):
            s_ref[b, 2 * p + i] = s_new[u][i]
    for u, (b, p) in enumerate(units):
        for i in range(2):
            h = 2 * p + i
            gate = cur_ref[b, :, nconv + h * GDN_DV:nconv + (h + 1) * GDN_DV].astype(F32)
            o_ref[b, :, h * GDN_DV:(h + 1) * GDN_DV] = (
                _rms(o[u][halves[i]], ong_ref[...]) * _silu(gate)).astype(o_ref.dtype)


def gdn_core(proj, ab, conv_w, a_log, dt_bias, o_norm_g):
    batch, seq_len, W = proj.shape
    nv = GDN_HEADS * GDN_DV
    nconv = conv_w.shape[1]
    nb = SEQS_PER_STEP
    pad = lambda a: jnp.pad(a.reshape(1, -1), ((0, 0), (0, LANES - a.shape[-1])))
    const = lambda shape: pl.BlockSpec(shape, lambda b, c: (0, 0))
    return pl.pallas_call(
        _gdn_kernel, grid=(batch // nb, seq_len // CHUNK),
        in_specs=[pl.BlockSpec((nb, CHUNK, W), lambda b, c: (b, c, 0)),
                  _prev_rows_spec(W),
                  pl.BlockSpec((nb, CHUNK, LANES), lambda b, c: (b, c, 0)),
                  const((CONV_W, nconv)), const((1, LANES)), const((1, LANES)),
                  const((1, GDN_DV))],
        out_specs=pl.BlockSpec((nb, CHUNK, nv), lambda b, c: (b, c, 0)),
        out_shape=jax.ShapeDtypeStruct((batch, seq_len, nv), BF16),
        scratch_shapes=[pltpu.VMEM((nb, HALO + CHUNK, nconv), BF16),
                        pltpu.VMEM((nb, GDN_HEADS, GDN_DK, GDN_DV), F32)],
        compiler_params=_cparams("parallel", "arbitrary"), name="gdn_core",
    )(proj, proj, ab, conv_w, pad(a_log), pad(dt_bias), o_norm_g.reshape(1, GDN_DV))


def gdn_mixer(h, g_mix, w_in_stack, layer, conv_w, a_log, dt_bias, o_norm_g, *, batch, seq_len):
    nmain = conv_w.shape[1] + GDN_HEADS * GDN_DV
    w_small = jnp.pad(w_in_stack[layer][:, nmain:], ((0, 0), (0, LANES - 2 * GDN_HEADS)))
    proj, ab = norm_matmul(h, g_mix, w_in_stack, layer, nmain, w_small, out_dtype=BF16)
    o = gdn_core(proj.reshape(batch, seq_len, -1), ab.reshape(batch, seq_len, -1),
                 conv_w, a_log, dt_bias, o_norm_g)
    return o.reshape(batch * seq_len, -1)


def _m2_kernel(cur_ref, prev_ref, dt_ref, cw_ref, cb_ref, dtb_ref, alog_ref, dsk_ref,
               ng_ref, sel_ref, o_ref, xb_ref, s_ref):
    c = pl.program_id(1)
    nseq = cur_ref.shape[0]
    seqs = range(nseq)
    inner = dsk_ref.shape[1]
    nbc = M2_GROUPS * M2_STATE
    pair_w = 2 * M2_HEAD_DIM
    pairs_per_group = inner // M2_GROUPS // pair_w
    gw = inner // M2_GROUPS

    @pl.when(c == 0)
    def _():
        s_ref[...] = jnp.zeros_like(s_ref)

    for b in seqs:
        _conv_window(cur_ref[b, :, inner:], prev_ref[b, :, inner:], xb_ref.at[b], c == 0)
    xbc = [_silu(_causal_conv(xb_ref.at[b], cw_ref) + cb_ref[...]) for b in seqs]
    x = [t[:, :inner] for t in xbc]

    causal, _ = _causal_masks(CHUNK)
    dt = [_softplus(dt_ref[b] + dtb_ref[...]) for b in seqs]
    cum = [_chunk_cumsum(t * (-jnp.exp(alog_ref[...])), causal) for t in dt]
    cum_rows = [jnp.concatenate([t, t], axis=0).T for t in cum]
    cum_hi, cum_lo = zip(*[_split2(t) for t in cum])
    scales = [a.astype(BF16) for b in seqs
              for a in (dt[b], jnp.exp(cum[b]), dt[b] * jnp.exp(cum[b][CHUNK - 1:CHUNK, :] - cum[b]))]
    spread = _dot(jnp.concatenate(list(cum_hi) + list(cum_lo) + scales, axis=0), sel_ref[...])
    block = lambda n: spread[n * CHUNK:(n + 1) * CHUNK]
    cum_x = [block(b) + block(nseq + b) for b in seqs]
    xdt = [x[b] * block(2 * nseq + 3 * b) for b in seqs]
    ecum = [block(2 * nseq + 3 * b + 1) for b in seqs]
    xd = [x[b] * block(2 * nseq + 3 * b + 2) for b in seqs]

    lane = lax.broadcasted_iota(jnp.int32, (CHUNK, pair_w), 1)
    row = lax.broadcasted_iota(jnp.int32, (CHUNK, pair_w), 0)
    first = lane < M2_HEAD_DIM
    causal2 = row >= jnp.where(first, lane, lane - M2_HEAD_DIM)

    units = [(b, g) for b in seqs for g in range(M2_GROUPS)]
    gcol = lambda g: slice(g * gw, (g + 1) * gw)
    Bg = [xbc[b][:, inner + g * M2_STATE:inner + (g + 1) * M2_STATE].astype(BF16) for b, g in units]
    Cg = [xbc[b][:, inner + nbc + g * M2_STATE:inner + nbc + (g + 1) * M2_STATE].astype(BF16)
          for b, g in units]
    cb2 = [_dot_nt(Cg[u], jnp.concatenate([Bg[u], Bg[u]], axis=0))
           for u in range(len(units))]
    S = [s_ref[b, g] for b, g in units]
    y_off = [_dot(Cg[u], S[u].astype(BF16)) * ecum[b][:, gcol(g)] for u, (b, g) in enumerate(units)]
    s_new = [ecum[b][CHUNK - 1:CHUNK, gcol(g)] * S[u]
             + _dot_tn(Bg[u], xd[b][:, gcol(g)].astype(BF16)) for u, (b, g) in enumerate(units)]
    for u, (b, g) in enumerate(units):
        s_ref[b, g] = s_new[u]
    y_diag = []
    for u, (b, g) in enumerate(units):
        for m in range(pairs_per_group):
            p = g * pairs_per_group + m
            cols = slice(p * pair_w, (p + 1) * pair_w)
            h1 = 2 * p
            row_pair = jnp.where(first[0:1], cum_rows[b][h1:h1 + 1, :], cum_rows[b][h1 + 1:h1 + 2, :])
            diff = cum_x[b][:, cols] - row_pair
            lmat = jnp.where(causal2, jnp.exp(jnp.where(causal2, diff, 0.0)), 0.0)
            xp = xdt[b][:, cols]
            x_bd = jnp.concatenate([jnp.where(first, xp, 0.0), jnp.where(first, 0.0, xp)], axis=0)
            y_diag.append(_dot((cb2[u] * lmat).astype(BF16), x_bd.astype(BF16)))
    for u, (b, g) in enumerate(units):
        cols = gcol(g)
        yd = jnp.concatenate(y_diag[u * pairs_per_group:(u + 1) * pairs_per_group], axis=1)
        y = ((yd + y_off[u] + dsk_ref[:, cols] * x[b][:, cols])
             * _silu(cur_ref[b, :, cols].astype(F32)))
        o_ref[b, :, cols] = _rms(y, ng_ref[:, cols]).astype(o_ref.dtype)


def m2_core(proj, dt_raw, conv_w, conv_b, dt_bias, a_log, d_skip, norm_g):
    batch, seq_len, W = proj.shape
    nconv = conv_w.shape[1]
    inner = W - nconv
    nb = SEQS_PER_STEP
    pad = lambda a: jnp.pad(a.reshape(1, -1), ((0, 0), (0, LANES - a.shape[-1])))
    head_of_lane = jnp.arange(inner, dtype=jnp.int32) // M2_HEAD_DIM
    sel = (jnp.arange(LANES, dtype=jnp.int32)[:, None] == head_of_lane[None, :]).astype(BF16)
    d_x = jnp.repeat(d_skip, M2_HEAD_DIM).reshape(1, inner)
    const = lambda shape: pl.BlockSpec(shape, lambda b, c: (0, 0))
    return pl.pallas_call(
        _m2_kernel, grid=(batch // nb, seq_len // CHUNK),
        in_specs=[pl.BlockSpec((nb, CHUNK, W), lambda b, c: (b, c, 0)),
                  _prev_rows_spec(W),
                  pl.BlockSpec((nb, CHUNK, LANES), lambda b, c: (b, c, 0)),
                  const((CONV_W, nconv)), const((1, nconv)), const((1, LANES)),
                  const((1, LANES)), const((1, inner)), const((1, inner)),
                  const((LANES, inner))],
        out_specs=pl.BlockSpec((nb, CHUNK, inner), lambda b, c: (b, c, 0)),
        out_shape=jax.ShapeDtypeStruct((batch, seq_len, inner), BF16),
        scratch_shapes=[pltpu.VMEM((nb, HALO + CHUNK, nconv), BF16),
                        pltpu.VMEM((nb, M2_GROUPS, M2_STATE, inner // M2_GROUPS), F32)],
        compiler_params=_cparams("parallel", "arbitrary"), name="m2_core",
    )(proj, proj, dt_raw, conv_w, conv_b.reshape(1, nconv), pad(dt_bias), pad(a_log),
      d_x, norm_g.reshape(1, inner), sel)


def m2_mixer(h, g_mix, w_in_stack, layer, conv_w, conv_b, dt_bias, a_log, d_skip, norm_g,
             *, batch, seq_len):
    heads = dt_bias.shape[0]
    nmain = w_in_stack.shape[2] - heads
    w_small = jnp.pad(w_in_stack[layer][:, nmain:], ((0, 0), (0, LANES - heads)))
    proj, dt_raw = norm_matmul(h, g_mix, w_in_stack, layer, nmain, w_small, out_dtype=BF16)
    y = m2_core(proj.reshape(batch, seq_len, -1), dt_raw.reshape(batch, seq_len, -1),
                conv_w, conv_b, dt_bias, a_log, d_skip, norm_g)
    return y.reshape(batch * seq_len, -1)


def _s5_prep_kernel(lr_ref, li_ref, ldt_ref, br_ref, bi_ref, lbr_ref, lbi_ref, bbr_ref, bbi_ref):
    lr, li = lr_ref[...], li_ref[...]
    dt = jnp.exp(ldt_ref[...])
    mag = jnp.exp(lr * dt)
    lbr = mag * jnp.cos(li * dt)
    lbi = mag * jnp.sin(li * dt)
    nr = lbr - 1.0
    den = lr * lr + li * li
    qr = (nr * lr + lbi * li) / den
    qi = (lbi * lr - nr * li) / den
    br, bi = br_ref[...], bi_ref[...]
    lbr_ref[...] = lbr
    lbi_ref[...] = lbi
    bbr_ref[...] = qr * br - qi * bi
    bbi_ref[...] = qr * bi + qi * br


def _s5_kernel(u_ref, b_ref, cre_ref, cim_ref, lam_ref, d_ref, y_ref, bu_ref, st_ref, *, steps):
    nslab = b_ref.shape[0]
    half = b_ref.shape[2] // 2

    @pl.when(pl.program_id(0) == 0)
    def _():
        st_ref[...] = jnp.zeros_like(st_ref)

    batch, _, d_model = u_ref.shape
    rows = batch * steps

    dst = lax.broadcasted_iota(jnp.int32, (rows, rows), 0)
    src = lax.broadcasted_iota(jnp.int32, (rows, rows), 1)
    perm = jnp.where((dst // batch == src % steps) & (dst % batch == src // steps),
                     1.0, 0.0).astype(BF16)
    u_hi, u_lo = _split2(u_ref[...].reshape(rows, d_model))
    u_hi = _dot(perm, u_hi)
    u_tm = u_hi + _dot(perm, u_lo)
    ub = u_hi.astype(BF16)

    for s in range(nslab):
        bu_ref[:, 2 * half * s:2 * half * (s + 1)] = _dot(
            ub[:, s * S5_SLAB:(s + 1) * S5_SLAB], b_ref[s])

    for s in range(nslab):
        re = slice(2 * half * s, 2 * half * s + half)
        im = slice(2 * half * s + half, 2 * half * (s + 1))
        lam_r = jnp.broadcast_to(lam_ref[:, re], (batch, half))
        lam_i = jnp.broadcast_to(lam_ref[:, im], (batch, half))

        def step(t, carry, re=re, im=im, lam_r=lam_r, lam_i=lam_i):
            xr, xi = carry
            at_t = pl.ds(pl.multiple_of(t * batch, batch), batch)
            nr = lam_r * xr - lam_i * xi + bu_ref[at_t, re]
            ni = lam_r * xi + lam_i * xr + bu_ref[at_t, im]
            bu_ref[at_t, re] = nr
            bu_ref[at_t, im] = ni
            return nr, ni

        xr, xi = lax.fori_loop(0, steps, step, (st_ref[:, re], st_ref[:, im]))
        st_ref[:, re] = xr
        st_ref[:, im] = xi

    ys = []
    for s in range(nslab):
        re = slice(2 * half * s, 2 * half * s + half)
        im = slice(2 * half * s + half, 2 * half * (s + 1))
        cols = slice(s * S5_SLAB, (s + 1) * S5_SLAB)
        y = (_dot(bu_ref[:, re].astype(BF16), cre_ref[s])
             - _dot(bu_ref[:, im].astype(BF16), cim_ref[s])
             + d_ref[:, cols] * u_tm[:, cols])
        ys.append(jax.nn.gelu(y, approximate=True).astype(BF16))
    y_sm = _dot_tn(perm, jnp.concatenate(ys, axis=1))
    y_ref[...] = y_sm.astype(y_ref.dtype).reshape(batch, steps, d_model)


def s5_core(u, lam_re, lam_im, log_dt, b_re, b_im, c_re, c_im, d_skip, *, steps=32):
    batch, L, D = u.shape
    assert batch == SUBLANES, "one time step's batch entries fill one sublane tile"
    steps = min(steps, L)
    G, P = lam_re.shape
    K = S5_GROUP
    nslab = D // S5_SLAB
    gps = S5_SLAB // K
    half = gps * P

    rep = lambda a: jnp.repeat(a, K, axis=-1)
    flat = jax.ShapeDtypeStruct((G, P * K), F32)
    lbr, lbi, bbr, bbi = pl.pallas_call(
        _s5_prep_kernel, out_shape=[flat] * 4, name="s5_prep",
    )(rep(lam_re), rep(lam_im), jnp.broadcast_to(log_dt[:, None], (G, P * K)),
      b_re.reshape(G, P * K), b_im.reshape(G, P * K))

    eye = jnp.eye(gps, dtype=F32)
    to_slab = lambda a: a.reshape(G, P, K).transpose(0, 2, 1).reshape(nslab, gps, K, P)
    b_blk = jnp.concatenate(
        [jnp.einsum('sjkp,jJ->sjkJp', to_slab(a), eye).reshape(nslab, S5_SLAB, half)
         for a in (bbr, bbi)], axis=-1).astype(BF16)
    c_blk = [jnp.einsum('sjkp,jJ->sJpjk', a.reshape(nslab, gps, K, P), eye)
             .reshape(nslab, half, S5_SLAB).astype(BF16) for a in (c_re, c_im)]
    lam = jnp.stack([lbr[:, ::K].reshape(nslab, half), lbi[:, ::K].reshape(nslab, half)],
                    axis=1).reshape(1, nslab * 2 * half)

    const3 = lambda shape: pl.BlockSpec(shape, lambda i: (0, 0, 0))
    return pl.pallas_call(
        functools.partial(_s5_kernel, steps=steps), grid=(L // steps,),
        in_specs=[pl.BlockSpec((batch, steps, D), lambda i: (0, i, 0)),
                  const3((nslab, S5_SLAB, 2 * half)),
                  const3((nslab, half, S5_SLAB)), const3((nslab, half, S5_SLAB)),
                  pl.BlockSpec((1, nslab * 2 * half), lambda i: (0, 0)),
                  pl.BlockSpec((1, D), lambda i: (0, 0))],
        out_specs=pl.BlockSpec((batch, steps, D), lambda i: (0, i, 0)),
        out_shape=jax.ShapeDtypeStruct((batch, L, D), BF16),
        scratch_shapes=[pltpu.VMEM((batch * steps, nslab * 2 * half), F32),
                        pltpu.VMEM((batch, nslab * 2 * half), F32)],
        compiler_params=_cparams("arbitrary"), name="s5_core",
    )(u, b_blk, c_blk[0], c_blk[1], lam, d_skip.reshape(1, D))


def s5_mixer(h, g_mix, w_in_stack, layer, lam_re, lam_im, log_dt, b_re, b_im, c_re, c_im, d_skip,
             *, batch, seq_len):
    T, D = h.shape
    u = norm_matmul(h, g_mix, w_in_stack, layer, D, out_dtype=F32)
    y = s5_core(u.reshape(batch, seq_len, D), lam_re, lam_im, log_dt, b_re, b_im, c_re, c_im,
                d_skip)
    return y.reshape(T, D)


def kernel(x, norm_mix_g, norm_mlp_g, mlp_w1, mlp_w2, gdn_w_in, gdn_conv_w, gdn_a_log, gdn_dt_bias, gdn_o_norm_g, gdn_w_out, s5_w_in, s5_lam_re, s5_lam_im, s5_log_dt, s5_b_re, s5_b_im, s5_c_re, s5_c_im, s5_d, s5_w_out, m2_w_in, m2_conv_w, m2_conv_b, m2_dt_bias, m2_a_log, m2_d, m2_norm_g, m2_w_out, final_norm_g):
    batch, seq_len, d_model = x.shape
    depth = norm_mix_g.shape[0]
    h = x.reshape(batch * seq_len, d_model)
    mlp_w1, mlp_w2, gdn_w_in, s5_w_in, m2_w_in = (
        w.astype(BF16) for w in (mlp_w1, mlp_w2, gdn_w_in, s5_w_in, m2_w_in))
    for i in range(depth):
        kind, j = i % 3, i // 3
        if kind == 0:
            mix = gdn_mixer(h, norm_mix_g[i], gdn_w_in, j, gdn_conv_w[j], gdn_a_log[j],
                            gdn_dt_bias[j], gdn_o_norm_g[j], batch=batch, seq_len=seq_len)
            w_out = gdn_w_out[j]
        elif kind == 1:
            mix = s5_mixer(h, norm_mix_g[i], s5_w_in, j, s5_lam_re[j], s5_lam_im[j], s5_log_dt[j],
                           s5_b_re[j], s5_b_im[j], s5_c_re[j], s5_c_im[j], s5_d[j],
                           batch=batch, seq_len=seq_len)
            w_out = s5_w_out[j]
        else:
            mix = m2_mixer(h, norm_mix_g[i], m2_w_in, j, m2_conv_w[j], m2_conv_b[j], m2_dt_bias[j],
                           m2_a_log[j], m2_d[j], m2_norm_g[j], batch=batch, seq_len=seq_len)
            w_out = m2_w_out[j]
        h = out_proj_mlp(mix, w_out.astype(BF16), h, norm_mlp_g[i], mlp_w1, mlp_w2, i,
                         final_norm_g if i == depth - 1 else None, glu=(kind == 1))
    return h.reshape(batch, seq_len, d_model)
```

```python
import functools

import jax
import jax.numpy as jnp
from jax import lax
from jax.experimental import pallas as pl
from jax.experimental.pallas import tpu as pltpu

F32 = jnp.float32
BF16 = jnp.bfloat16

RMS_EPS = 1e-6
L2_EPS = 1e-6
CHUNK = 64
CONV_W = 4
LANES = 128
SUBLANES = 8
HALO = 2 * SUBLANES
CONV_COLS = 512
SEQS_PER_STEP = 4
VMEM_LIMIT = 56 * 1024 * 1024

GDN_HEADS = 8
GDN_DK = 128
GDN_DV = 128
S5_GROUP = 16
S5_STATE = 64
S5_SLAB = 128
M2_HEAD_DIM = 64
M2_STATE = 128
M2_GROUPS = 8


def _cparams(*sem):
    return pltpu.CompilerParams(dimension_semantics=sem, vmem_limit_bytes=VMEM_LIMIT)


def _rms(x, g):
    return x * lax.rsqrt(jnp.mean(x * x, axis=-1, keepdims=True) + RMS_EPS) * g


def _sigmoid(x):
    return 0.5 * (jnp.tanh(0.5 * x) + 1.0)


def _silu(x):
    hx = 0.5 * x
    return hx + hx * jnp.tanh(hx)


def _softplus(x):
    return jnp.maximum(x, 0.0) + jnp.log(1.0 + jnp.exp(-jnp.abs(x)))


def _dot(a, b):
    return jnp.dot(a, b, preferred_element_type=F32)


def _split2(x):
    hi = x.astype(BF16)
    return hi, (x - hi.astype(F32)).astype(BF16)


def _split3(x):
    hi = x.astype(BF16)
    r = x - hi.astype(F32)
    mid = r.astype(BF16)
    return hi, mid, (r - mid.astype(F32)).astype(BF16)


def _dot_nt(a, b):
    return lax.dot_general(a, b, (((1,), (1,)), ((), ())), preferred_element_type=F32)


def _dot_tn(a, b):
    return lax.dot_general(a, b, (((0,), (0,)), ((), ())), preferred_element_type=F32)


def _causal_masks(n):
    row = lax.broadcasted_iota(jnp.int32, (n, n), 0)
    col = lax.broadcasted_iota(jnp.int32, (n, n), 1)
    return row >= col, row > col


def _norm_mm_kernel(x_ref, g_ref, w_ref, *rest, small):
    if small:
        ws_ref, o_ref, os_ref, xn_ref = rest
    else:
        o_ref, xn_ref = rest

    @pl.when(pl.program_id(1) == 0)
    def _():
        xn = _rms(x_ref[...], g_ref[...]).astype(BF16)
        xn_ref[...] = xn
        if small:
            os_ref[...] = _dot(xn, ws_ref[...])

    o_ref[...] = _dot(xn_ref[...], w_ref[...]).astype(o_ref.dtype)


def norm_matmul(h, g, w_stack, layer, n_out, w_small=None, *, out_dtype, tm=1024, tn=2048):
    T, D = h.shape
    N = n_out
    tm = min(tm, T)
    tn = min(tn, N)
    ni, nj = T // tm, N // tn
    x_spec = pl.BlockSpec((tm, D), lambda i, j: (i, 0))
    g_spec = pl.BlockSpec((1, D), lambda i, j: (0, 0))
    w_spec = pl.BlockSpec((None, D, tn), lambda i, j: (layer, 0, j))
    out_shape = jax.ShapeDtypeStruct((T, N), out_dtype)
    o_spec = pl.BlockSpec((tm, tn), lambda i, j: (i, j))
    in_specs = [x_spec, g_spec, w_spec]
    args = [h, g.reshape(1, D), w_stack]
    small = w_small is not None
    if small:
        ns = w_small.shape[1]
        in_specs.append(pl.BlockSpec((D, ns), lambda i, j: (0, 0)))
        args.append(w_small)
        o_spec = [o_spec, pl.BlockSpec((tm, ns), lambda i, j: (i, 0))]
        out_shape = [out_shape, jax.ShapeDtypeStruct((T, ns), F32)]
    return pl.pallas_call(
        functools.partial(_norm_mm_kernel, small=small), grid=(ni, nj), in_specs=in_specs,
        out_specs=o_spec, out_shape=out_shape,
        scratch_shapes=[pltpu.VMEM((tm, D), BF16)],
        compiler_params=_cparams("parallel", "arbitrary"), name="norm_matmul",
    )(*args)


def _out_mlp_kernel(mix_ref, wo_ref, r_ref, g_ref, w1_ref, w2_ref, *rest, glu, final):
    if final:
        gf_ref, o_ref, xn_ref = rest
    else:
        o_ref, xn_ref = rest
    j = pl.program_id(1)

    @pl.when(j == 0)
    def _():
        m = _dot(mix_ref[...], wo_ref[...])
        if glu:
            n = o_ref.shape[1]
            m = m[:, :n] * _sigmoid(m[:, n:])
        h1 = r_ref[...] + m
        o_ref[...] = h1
        xn_ref[...] = _rms(h1, g_ref[...]).astype(BF16)

    a = jnp.maximum(_dot(xn_ref[...], w1_ref[...]), 0.0)
    o_ref[...] += _dot((a * a).astype(BF16), w2_ref[...])

    if final:
        @pl.when(j == pl.num_programs(1) - 1)
        def _():
            o_ref[...] = _rms(o_ref[...], gf_ref[...])


def out_proj_mlp(mix, w_out, res, g, w1_stack, w2_stack, layer, g_final=None, *,
                 glu=False, tm=1024, tf=1024):
    T, K = mix.shape
    D = res.shape[1]
    FF = w1_stack.shape[2]
    tm = min(tm, T)
    final = g_final is not None
    in_specs = [pl.BlockSpec((tm, K), lambda i, j: (i, 0)),
                pl.BlockSpec(w_out.shape, lambda i, j: (0, 0)),
                pl.BlockSpec((tm, D), lambda i, j: (i, 0)),
                pl.BlockSpec((1, D), lambda i, j: (0, 0)),
                pl.BlockSpec((None, D, tf), lambda i, j: (layer, 0, j)),
                pl.BlockSpec((None, tf, D), lambda i, j: (layer, j, 0))]
    args = [mix, w_out, res, g.reshape(1, D), w1_stack, w2_stack]
    if final:
        in_specs.append(pl.BlockSpec((1, D), lambda i, j: (0, 0)))
        args.append(g_final.reshape(1, D))
    return pl.pallas_call(
        functools.partial(_out_mlp_kernel, glu=glu, final=final), grid=(T // tm, FF // tf),
        in_specs=in_specs,
        out_specs=pl.BlockSpec((tm, D), lambda i, j: (i, 0)),
        out_shape=jax.ShapeDtypeStruct((T, D), F32),
        scratch_shapes=[pltpu.VMEM((tm, D), BF16)],
        compiler_params=_cparams("parallel", "arbitrary"), name="out_proj_mlp",
    )(*args)


def _conv_window(cur, prev, xb_ref, first_chunk):
    xb_ref[0:HALO, :] = jnp.where(first_chunk, jnp.zeros_like(prev), prev)
    xb_ref[HALO:HALO + CHUNK, :] = cur


def _causal_conv(xb_ref, cw_ref):
    i = lax.broadcasted_iota(jnp.int32, (CHUNK, HALO + CHUNK), 0)
    c = lax.broadcasted_iota(jnp.int32, (CHUNK, HALO + CHUNK), 1)
    shift = jnp.concatenate(
        [jnp.where(c == i + (HALO - (CONV_W - 1) + w), 1.0, 0.0) for w in range(CONV_W - 1)],
        axis=0).astype(BF16)
    outs = []
    for c0 in range(0, xb_ref.shape[1], CONV_COLS):
        cols = slice(c0, c0 + CONV_COLS)
        shifted = _dot(shift, xb_ref[:, cols])
        acc = xb_ref[HALO:HALO + CHUNK, cols].astype(F32) * cw_ref[CONV_W - 1:CONV_W, cols]
        for w in range(CONV_W - 1):
            acc = acc + shifted[w * CHUNK:(w + 1) * CHUNK] * cw_ref[w:w + 1, cols]
        outs.append(acc)
    return jnp.concatenate(outs, axis=1)


def _chunk_cumsum(x, causal):
    n = x.shape[1]
    parts = _dot(jnp.where(causal, 1.0, 0.0).astype(BF16), jnp.concatenate(_split3(x), axis=1))
    return parts[:, :n] + (parts[:, n:2 * n] + parts[:, 2 * n:])


def _prev_rows_spec(width):
    blocks_per_chunk = CHUNK // HALO
    return pl.BlockSpec(
        (SEQS_PER_STEP, HALO, width),
        lambda b, c: (b, jnp.maximum(c * blocks_per_chunk - 1, 0), 0))


def _dot_split(a, b):
    return _dot(jnp.concatenate([a[0], a[0], a[1]], axis=1),
                jnp.concatenate([b[0], b[1], b[0]], axis=0))


def _unit_lower_inverses(mats, nilpotent):
    n = mats[0].shape[0]
    row = lax.broadcasted_iota(jnp.int32, (n, n), 0)
    col = lax.broadcasted_iota(jnp.int32, (n, n), 1)
    eye = jnp.where(row == col, 1.0, 0.0)
    ps = [eye - a for a in mats]
    aks = [a.astype(BF16) for a in mats]
    power = 2
    while power < nilpotent:
        aks = [_dot(ak, ak).astype(BF16) for ak in aks]
        ps = [p + _dot(p.astype(BF16), ak) for p, ak in zip(ps, aks)]
        power *= 2
    resid = [eye - _dot_split(_split2(eye + a), _split2(p)) for a, p in zip(mats, ps)]
    return [p + _dot(p.astype(BF16), r.astype(BF16)) for p, r in zip(ps, resid)]


def _gdn_kernel(cur_ref, prev_ref, ab_ref, cw_ref, alog_ref, dtb_ref, ong_ref,
                o_ref, xb_ref, s_ref):
    c = pl.program_id(1)
    nseq = cur_ref.shape[0]
    nqk = GDN_HEADS * GDN_DK
    nconv = 2 * nqk + GDN_HEADS * GDN_DV
    units = [(b, p) for b in range(nseq) for p in range(GDN_HEADS // 2)]
    nu = range(len(units))
    stack = lambda f, p: jnp.concatenate([f(2 * p), f(2 * p + 1)], axis=0)

    @pl.when(c == 0)
    def _():
        s_ref[...] = jnp.zeros_like(s_ref)

    for b in range(nseq):
        _conv_window(cur_ref[b, :, :nconv], prev_ref[b, :, :nconv], xb_ref.at[b], c == 0)
    qkv = [_silu(_causal_conv(xb_ref.at[b], cw_ref)) for b in range(nseq)]

    causal, _ = _causal_masks(CHUNK)
    row = lax.broadcasted_iota(jnp.int32, (2 * CHUNK, 2 * CHUNK), 0)
    lane = lax.broadcasted_iota(jnp.int32, (2 * CHUNK, 2 * CHUNK), 1)
    first = lane < CHUNK
    same_head = (row < CHUNK) == first
    t_row = jnp.where(row < CHUNK, row, row - CHUNK)
    t_col = jnp.where(first, lane, lane - CHUNK)
    causal2 = same_head & (t_row >= t_col)
    strict2 = same_head & (t_row > t_col)

    ab = [ab_ref[b] for b in range(nseq)]
    beta_all = [_sigmoid(t) for t in ab]
    G = [_chunk_cumsum(-jnp.exp(alog_ref[...]) * _softplus(t + dtb_ref[...]), causal) for t in ab]
    G_rows = [jnp.concatenate([t, t], axis=0).T for t in G]
    eG = [jnp.exp(t) for t in G]
    eG_rev = [jnp.exp(t[CHUNK - 1:CHUNK, :] - t) for t in G]

    def heads_of(width, offset, b, p):
        return stack(lambda h: qkv[b][:, offset + h * width:offset + (h + 1) * width], p)

    q = [heads_of(GDN_DK, 0, b, p) for b, p in units]
    k = [heads_of(GDN_DK, nqk, b, p) for b, p in units]
    v = [heads_of(GDN_DV, 2 * nqk, b, p) for b, p in units]
    squares = jnp.concatenate([t * t for t in q + k], axis=0).astype(BF16)
    inv_norm = lax.rsqrt(_dot(squares, jnp.ones((GDN_DK, GDN_DK), BF16)) + L2_EPS)
    rows = 2 * CHUNK
    q = [q[u] * inv_norm[u * rows:(u + 1) * rows] * (GDN_DK ** -0.5) for u in nu]
    k = [k[u] * inv_norm[(len(units) + u) * rows:(len(units) + u + 1) * rows] for u in nu]
    col_of = lambda a, b, p, off=0: stack(lambda h: a[b][:, off + h:off + h + 1], p)
    beta = [col_of(beta_all, b, p, GDN_HEADS) for b, p in units]
    eg = [col_of(eG, b, p) for b, p in units]
    decay = []
    for b, p in units:
        g_row = jnp.where(first[0:1], G_rows[b][2 * p:2 * p + 1, :], G_rows[b][2 * p + 1:2 * p + 2, :])
        diff = col_of(G, b, p) - g_row
        decay.append(jnp.where(causal2, jnp.exp(jnp.where(causal2, diff, 0.0)), 0.0))
    kb = [t.astype(BF16) for t in k]
    qkk = [_dot_nt(jnp.concatenate([q[u].astype(BF16), kb[u]], axis=0), kb[u]) for u in nu]
    t_inv = _unit_lower_inverses(
        [jnp.where(strict2, beta[u] * qkk[u][rows:] * decay[u], 0.0) for u in nu], CHUNK)
    uw = [_dot(t_inv[u].astype(BF16),
               jnp.concatenate([v[u] * beta[u], k[u] * (beta[u] * eg[u])], axis=1).astype(BF16))
          for u in nu]
    qe = [q[u] * eg[u] for u in nu]
    halves = (slice(0, CHUNK), slice(CHUNK, rows))
    S = [[s_ref[b, 2 * p + i] for i in range(2)] for b, p in units]
    Sb = [[t.astype(BF16) for t in pair] for pair in S]
    ws = [[_dot(jnp.concatenate([uw[u][halves[i], GDN_DV:], qe[u][halves[i]]], axis=0).astype(BF16),
                Sb[u][i]) for i in range(2)] for u in nu]
    vb = [(uw[u][:, :GDN_DV] - jnp.concatenate([ws[u][0][:CHUNK], ws[u][1][:CHUNK]], axis=0))
          .astype(BF16) for u in nu]
    o = [jnp.concatenate([ws[u][0][CHUNK:], ws[u][1][CHUNK:]], axis=0)
         + _dot((qkk[u][:rows] * decay[u]).astype(BF16), vb[u]) for u in nu]
    k_dec = [(k[u] * col_of(eG_rev, b, p)).astype(BF16) for u, (b, p) in enumerate(units)]
    s_new = [[eG[b][CHUNK - 1:CHUNK, 2 * p + i:2 * p + i + 1] * S[u][i]
              + _dot_tn(k_dec[u][halves[i]], vb[u][halves[i]]) for i in range(2)]
             for u, (b, p) in enumerate(units)]
    for u, (b, p) in enumerate(units):
        for i in range(2):
            s_ref[b, 2 * p + i] = s_new[u][i]
    for u, (b, p) in enumerate(units):
        for i in range(2):
            h = 2 * p + i
            gate = cur_ref[b, :, nconv + h * GDN_DV:nconv + (h + 1) * GDN_DV].astype(F32)
            o_ref[b, :, h * GDN_DV:(h + 1) * GDN_DV] = (
                _rms(o[u][halves[i]], ong_ref[...]) * _silu(gate)).astype(o_ref.dtype)


def gdn_core(proj, ab, conv_w, a_log, dt_bias, o_norm_g):
    batch, seq_len, W = proj.shape
    nv = GDN_HEADS * GDN_DV
    nconv = conv_w.shape[1]
    nb = SEQS_PER_STEP
    pad = lambda a: jnp.pad(a.reshape(1, -1), ((0, 0), (0, LANES - a.shape[-1])))
    const = lambda shape: pl.BlockSpec(shape, lambda b, c: (0, 0))
    return pl.pallas_call(
        _gdn_kernel, grid=(batch // nb, seq_len // CHUNK),
        in_specs=[pl.BlockSpec((nb, CHUNK, W), lambda b, c: (b, c, 0)),
                  _prev_rows_spec(W),
                  pl.BlockSpec((nb, CHUNK, LANES), lambda b, c: (b, c, 0)),
                  const((CONV_W, nconv)), const((1, LANES)), const((1, LANES)),
                  const((1, GDN_DV))],
        out_specs=pl.BlockSpec((nb, CHUNK, nv), lambda b, c: (b, c, 0)),
        out_shape=jax.ShapeDtypeStruct((batch, seq_len, nv), BF16),
        scratch_shapes=[pltpu.VMEM((nb, HALO + CHUNK, nconv), BF16),
                        pltpu.VMEM((nb, GDN_HEADS, GDN_DK, GDN_DV), F32)],
        compiler_params=_cparams("parallel", "arbitrary"), name="gdn_core",
    )(proj, proj, ab, conv_w, pad(a_log), pad(dt_bias), o_norm_g.reshape(1, GDN_DV))


def gdn_mixer(h, g_mix, w_in_stack, layer, conv_w, a_log, dt_bias, o_norm_g, *, batch, seq_len):
    nmain = conv_w.shape[1] + GDN_HEADS * GDN_DV
    w_small = jnp.pad(w_in_stack[layer][:, nmain:], ((0, 0), (0, LANES - 2 * GDN_HEADS)))
    proj, ab = norm_matmul(h, g_mix, w_in_stack, layer, nmain, w_small, out_dtype=BF16)
    o = gdn_core(proj.reshape(batch, seq_len, -1), ab.reshape(batch, seq_len, -1),
                 conv_w, a_log, dt_bias, o_norm_g)
    return o.reshape(batch * seq_len, -1)


def _m2_kernel(cur_ref, prev_ref, dt_ref, cw_ref, cb_ref, dtb_ref, alog_ref, dsk_ref,
               ng_ref, sel_ref, o_ref, xb_ref, s_ref):
    c = pl.program_id(1)
    nseq = cur_ref.shape[0]
    seqs = range(nseq)
    inner = dsk_ref.shape[1]
    nbc = M2_GROUPS * M2_STATE
    pair_w = 2 * M2_HEAD_DIM
    pairs_per_group = inner // M2_GROUPS // pair_w
    gw = inner // M2_GROUPS

    @pl.when(c == 0)
    def _():
        s_ref[...] = jnp.zeros_like(s_ref)

    for b in seqs:
        _conv_window(cur_ref[b, :, inner:], prev_ref[b, :, inner:], xb_ref.at[b], c == 0)
    xbc = [_silu(_causal_conv(xb_ref.at[b], cw_ref) + cb_ref[...]) for b in seqs]
    x = [t[:, :inner] for t in xbc]

    causal, _ = _causal_masks(CHUNK)
    dt = [_softplus(dt_ref[b] + dtb_ref[...]) for b in seqs]
    cum = [_chunk_cumsum(t * (-jnp.exp(alog_ref[...])), causal) for t in dt]
    cum_rows = [jnp.concatenate([t, t], axis=0).T for t in cum]
    cum_hi, cum_lo = zip(*[_split2(t) for t in cum])
    scales = [a.astype(BF16) for b in seqs
              for a in (dt[b], jnp.exp(cum[b]), dt[b] * jnp.exp(cum[b][CHUNK - 1:CHUNK, :] - cum[b]))]
    spread = _dot(jnp.concatenate(list(cum_hi) + list(cum_lo) + scales, axis=0), sel_ref[...])
    block = lambda n: spread[n * CHUNK:(n + 1) * CHUNK]
    cum_x = [block(b) + block(nseq + b) for b in seqs]
    xdt = [x[b] * block(2 * nseq + 3 * b) for b in seqs]
    ecum = [block(2 * nseq + 3 * b + 1) for b in seqs]
    xd = [x[b] * block(2 * nseq + 3 * b + 2) for b in seqs]

    lane = lax.broadcasted_iota(jnp.int32, (CHUNK, pair_w), 1)
    row = lax.broadcasted_iota(jnp.int32, (CHUNK, pair_w), 0)
    first = lane < M2_HEAD_DIM
    causal2 = row >= jnp.where(first, lane, lane - M2_HEAD_DIM)

    units = [(b, g) for b in seqs for g in range(M2_GROUPS)]
    gcol = lambda g: slice(g * gw, (g + 1) * gw)
    Bg = [xbc[b][:, inner + g * M2_STATE:inner + (g + 1) * M2_STATE].astype(BF16) for b, g in units]
    Cg = [xbc[b][:, inner + nbc + g * M2_STATE:inner + nbc + (g + 1) * M2_STATE].astype(BF16)
          for b, g in units]
    cb2 = [_dot_nt(Cg[u], jnp.concatenate([Bg[u], Bg[u]], axis=0))
           for u in range(len(units))]
    S = [s_ref[b, g] for b, g in units]
    y_off = [_dot(Cg[u], S[u].astype(BF16)) * ecum[b][:, gcol(g)] for u, (b, g) in enumerate(units)]
    s_new = [ecum[b][CHUNK - 1:CHUNK, gcol(g)] * S[u]
             + _dot_tn(Bg[u], xd[b][:, gcol(g)].astype(BF16)) for u, (b, g) in enumerate(units)]
    for u, (b, g) in enumerate(units):
        s_ref[b, g] = s_new[u]
    y_diag = []
    for u, (b, g) in enumerate(units):
        for m in range(pairs_per_group):
            p = g * pairs_per_group + m
            cols = slice(p * pair_w, (p + 1) * pair_w)
            h1 = 2 * p
            row_pair = jnp.where(first[0:1], cum_rows[b][h1:h1 + 1, :], cum_rows[b][h1 + 1:h1 + 2, :])
            diff = cum_x[b][:, cols] - row_pair
            lmat = jnp.where(causal2, jnp.exp(jnp.where(causal2, diff, 0.0)), 0.0)
            xp = xdt[b][:, cols]
            x_bd = jnp.concatenate([jnp.where(first, xp, 0.0), jnp.where(first, 0.0, xp)], axis=0)
            y_diag.append(_dot((cb2[u] * lmat).astype(BF16), x_bd.astype(BF16)))
    for u, (b, g) in enumerate(units):
        cols = gcol(g)
        yd = jnp.concatenate(y_diag[u * pairs_per_group:(u + 1) * pairs_per_group], axis=1)
        y = ((yd + y_off[u] + dsk_ref[:, cols] * x[b][:, cols])
             * _silu(cur_ref[b, :, cols].astype(F32)))
        o_ref[b, :, cols] = _rms(y, ng_ref[:, cols]).astype(o_ref.dtype)


def m2_core(proj, dt_raw, conv_w, conv_b, dt_bias, a_log, d_skip, norm_g):
    batch, seq_len, W = proj.shape
    nconv = conv_w.shape[1]
    inner = W - nconv
    nb = SEQS_PER_STEP
    pad = lambda a: jnp.pad(a.reshape(1, -1), ((0, 0), (0, LANES - a.shape[-1])))
    head_of_lane = jnp.arange(inner, dtype=jnp.int32) // M2_HEAD_DIM
    sel = (jnp.arange(LANES, dtype=jnp.int32)[:, None] == head_of_lane[None, :]).astype(BF16)
    d_x = jnp.repeat(d_skip, M2_HEAD_DIM).reshape(1, inner)
    const = lambda shape: pl.BlockSpec(shape, lambda b, c: (0, 0))
    return pl.pallas_call(
        _m2_kernel, grid=(batch // nb, seq_len // CHUNK),
        in_specs=[pl.BlockSpec((nb, CHUNK, W), lambda b, c: (b, c, 0)),
                  _prev_rows_spec(W),
                  pl.BlockSpec((nb, CHUNK, LANES), lambda b, c: (b, c, 0)),
                  const((CONV_W, nconv)), const((1, nconv)), const((1, LANES)),
                  const((1, LANES)), const((1, inner)), const((1, inner)),
                  const((LANES, inner))],
        out_specs=pl.BlockSpec((nb, CHUNK, inner), lambda b, c: (b, c, 0)),
        out_shape=jax.ShapeDtypeStruct((batch, seq_len, inner), BF16),
        scratch_shapes=[pltpu.VMEM((nb, HALO + CHUNK, nconv), BF16),
                        pltpu.VMEM((nb, M2_GROUPS, M2_STATE, inner // M2_GROUPS), F32)],
        compiler_params=_cparams("parallel", "arbitrary"), name="m2_core",
    )(proj, proj, dt_raw, conv_w, conv_b.reshape(1, nconv), pad(dt_bias), pad(a_log),
      d_x, norm_g.reshape(1, inner), sel)


def m2_mixer(h, g_mix, w_in_stack, layer, conv_w, conv_b, dt_bias, a_log, d_skip, norm_g,
             *, batch, seq_len):
    heads = dt_bias.shape[0]
    nmain = w_in_stack.shape[2] - heads
    w_small = jnp.pad(w_in_stack[layer][:, nmain:], ((0, 0), (0, LANES - heads)))
    proj, dt_raw = norm_matmul(h, g_mix, w_in_stack, layer, nmain, w_small, out_dtype=BF16)
    y = m2_core(proj.reshape(batch, seq_len, -1), dt_raw.reshape(batch, seq_len, -1),
                conv_w, conv_b, dt_bias, a_log, d_skip, norm_g)
    return y.reshape(batch * seq_len, -1)


def _s5_prep_kernel(lr_ref, li_ref, ldt_ref, br_ref, bi_ref, lbr_ref, lbi_ref, bbr_ref, bbi_ref):
    lr, li = lr_ref[...], li_ref[...]
    dt = jnp.exp(ldt_ref[...])
    mag = jnp.exp(lr * dt)
    lbr = mag * jnp.cos(li * dt)
    lbi = mag * jnp.sin(li * dt)
    nr = lbr - 1.0
    den = lr * lr + li * li
    qr = (nr * lr + lbi * li) / den
    qi = (lbi * lr - nr * li) / den
    br, bi = br_ref[...], bi_ref[...]
    lbr_ref[...] = lbr
    lbi_ref[...] = lbi
    bbr_ref[...] = qr * br - qi * bi
    bbi_ref[...] = qr * bi + qi * br


def _s5_kernel(u_ref, b_ref, cre_ref, cim_ref, lam_ref, d_ref, y_ref, bu_ref, st_ref, *, steps):
    nslab = b_ref.shape[0]
    half = b_ref.shape[2] // 2

    @pl.when(pl.program_id(0) == 0)
    def _():
        st_ref[...] = jnp.zeros_like(st_ref)

    batch, _, d_model = u_ref.shape
    rows = batch * steps

    dst = lax.broadcasted_iota(jnp.int32, (rows, rows), 0)
    src = lax.broadcasted_iota(jnp.int32, (rows, rows), 1)
    perm = jnp.where((dst // batch == src % steps) & (dst % batch == src // steps),
                     1.0, 0.0).astype(BF16)
    u_hi, u_lo = _split2(u_ref[...].reshape(rows, d_model))
    u_hi = _dot(perm, u_hi)
    u_tm = u_hi + _dot(perm, u_lo)
    ub = u_hi.astype(BF16)

    for s in range(nslab):
        bu_ref[:, 2 * half * s:2 * half * (s + 1)] = _dot(
            ub[:, s * S5_SLAB:(s + 1) * S5_SLAB], b_ref[s])

    for s in range(nslab):
        re = slice(2 * half * s, 2 * half * s + half)
        im = slice(2 * half * s + half, 2 * half * (s + 1))
        lam_r = jnp.broadcast_to(lam_ref[:, re], (batch, half))
        lam_i = jnp.broadcast_to(lam_ref[:, im], (batch, half))

        def step(t, carry, re=re, im=im, lam_r=lam_r, lam_i=lam_i):
            xr, xi = carry
            at_t = pl.ds(pl.multiple_of(t * batch, batch), batch)
            nr = lam_r * xr - lam_i * xi + bu_ref[at_t, re]
            ni = lam_r * xi + lam_i * xr + bu_ref[at_t, im]
            bu_ref[at_t, re] = nr
            bu_ref[at_t, im] = ni
            return nr, ni

        xr, xi = lax.fori_loop(0, steps, step, (st_ref[:, re], st_ref[:, im]))
        st_ref[:, re] = xr
        st_ref[:, im] = xi

    ys = []
    for s in range(nslab):
        re = slice(2 * half * s, 2 * half * s + half)
        im = slice(2 * half * s + half, 2 * half * (s + 1))
        cols = slice(s * S5_SLAB, (s + 1) * S5_SLAB)
        y = (_dot(bu_ref[:, re].astype(BF16), cre_ref[s])
             - _dot(bu_ref[:, im].astype(BF16), cim_ref[s])
             + d_ref[:, cols] * u_tm[:, cols])
        ys.append(jax.nn.gelu(y, approximate=True).astype(BF16))
    y_sm = _dot_tn(perm, jnp.concatenate(ys, axis=1))
    y_ref[...] = y_sm.astype(y_ref.dtype).reshape(batch, steps, d_model)


def s5_core(u, lam_re, lam_im, log_dt, b_re, b_im, c_re, c_im, d_skip, *, steps=32):
    batch, L, D = u.shape
    assert batch == SUBLANES, "one time step's batch entries fill one sublane tile"
    steps = min(steps, L)
    G, P = lam_re.shape
    K = S5_GROUP
    nslab = D // S5_SLAB
    gps = S5_SLAB // K
    half = gps * P

    rep = lambda a: jnp.repeat(a, K, axis=-1)
    flat = jax.ShapeDtypeStruct((G, P * K), F32)
    lbr, lbi, bbr, bbi = pl.pallas_call(
        _s5_prep_kernel, out_shape=[flat] * 4, name="s5_prep",
    )(rep(lam_re), rep(lam_im), jnp.broadcast_to(log_dt[:, None], (G, P * K)),
      b_re.reshape(G, P * K), b_im.reshape(G, P * K))

    eye = jnp.eye(gps, dtype=F32)
    to_slab = lambda a: a.reshape(G, P, K).transpose(0, 2, 1).reshape(nslab, gps, K, P)
    b_blk = jnp.concatenate(
        [jnp.einsum('sjkp,jJ->sjkJp', to_slab(a), eye).reshape(nslab, S5_SLAB, half)
         for a in (bbr, bbi)], axis=-1).astype(BF16)
    c_blk = [jnp.einsum('sjkp,jJ->sJpjk', a.reshape(nslab, gps, K, P), eye)
             .reshape(nslab, half, S5_SLAB).astype(BF16) for a in (c_re, c_im)]
    lam = jnp.stack([lbr[:, ::K].reshape(nslab, half), lbi[:, ::K].reshape(nslab, half)],
                    axis=1).reshape(1, nslab * 2 * half)

    const3 = lambda shape: pl.BlockSpec(shape, lambda i: (0, 0, 0))
    return pl.pallas_call(
        functools.partial(_s5_kernel, steps=steps), grid=(L // steps,),
        in_specs=[pl.BlockSpec((batch, steps, D), lambda i: (0, i, 0)),
                  const3((nslab, S5_SLAB, 2 * half)),
                  const3((nslab, half, S5_SLAB)), const3((nslab, half, S5_SLAB)),
                  pl.BlockSpec((1, nslab * 2 * half), lambda i: (0, 0)),
                  pl.BlockSpec((1, D), lambda i: (0, 0))],
        out_specs=pl.BlockSpec((batch, steps, D), lambda i: (0, i, 0)),
        out_shape=jax.ShapeDtypeStruct((batch, L, D), BF16),
        scratch_shapes=[pltpu.VMEM((batch * steps, nslab * 2 * half), F32),
                        pltpu.VMEM((batch, nslab * 2 * half), F32)],
        compiler_params=_cparams("arbitrary"), name="s5_core",
    )(u, b_blk, c_blk[0], c_blk[1], lam, d_skip.reshape(1, D))


def s5_mixer(h, g_mix, w_in_stack, layer, lam_re, lam_im, log_dt, b_re, b_im, c_re, c_im, d_skip,
             *, batch, seq_len):
    T, D = h.shape
    u = norm_matmul(h, g_mix, w_in_stack, layer, D, out_dtype=F32)
    y = s5_core(u.reshape(batch, seq_len, D), lam_re, lam_im, log_dt, b_re, b_im, c_re, c_im,
                d_skip)
    return y.reshape(T, D)


def kernel(x, norm_mix_g, norm_mlp_g, mlp_w1, mlp_w2, gdn_w_in, gdn_conv_w, gdn_a_log, gdn_dt_bias, gdn_o_norm_g, gdn_w_out, s5_w_in, s5_lam_re, s5_lam_im, s5_log_dt, s5_b_re, s5_b_im, s5_c_re, s5_c_im, s5_d, s5_w_out, m2_w_in, m2_conv_w, m2_conv_b, m2_dt_bias, m2_a_log, m2_d, m2_norm_g, m2_w_out, final_norm_g):
    batch, seq_len, d_model = x.shape
    depth = norm_mix_g.shape[0]
    h = x.reshape(batch * seq_len, d_model)
    mlp_w1, mlp_w2, gdn_w_in, s5_w_in, m2_w_in = (
        w.astype(BF16) for w in (mlp_w1, mlp_w2, gdn_w_in, s5_w_in, m2_w_in))
    for i in range(depth):
        kind, j = i % 3, i // 3
        if kind == 0:
            mix = gdn_mixer(h, norm_mix_g[i], gdn_w_in, j, gdn_conv_w[j], gdn_a_log[j],
                            gdn_dt_bias[j], gdn_o_norm_g[j], batch=batch, seq_len=seq_len)
            w_out = gdn_w_out[j]
        elif kind == 1:
            mix = s5_mixer(h, norm_mix_g[i], s5_w_in, j, s5_lam_re[j], s5_lam_im[j], s5_log_dt[j],
                           s5_b_re[j], s5_b_im[j], s5_c_re[j], s5_c_im[j], s5_d[j],
                           batch=batch, seq_len=seq_len)
            w_out = s5_w_out[j]
        else:
            mix = m2_mixer(h, norm_mix_g[i], m2_w_in, j, m2_conv_w[j], m2_conv_b[j], m2_dt_bias[j],
                           m2_a_log[j], m2_d[j], m2_norm_g[j], batch=batch, seq_len=seq_len)
            w_out = m2_w_out[j]
        h = out_proj_mlp(mix, w_out.astype(BF16), h, norm_mlp_g[i], mlp_w1, mlp_w2, i,
                         final_norm_g if i == depth - 1 else None, glu=(kind == 1))
    return h.reshape(batch, seq_len, d_model)
```

```python
import functools

import jax
import jax.numpy as jnp
from jax import lax
from jax.experimental import pallas as pl
from jax.experimental.pallas import tpu as pltpu

F32 = jnp.float32
BF16 = jnp.bfloat16

RMS_EPS = 1e-6
L2_EPS = 1e-6
CHUNK = 64
CONV_W = 4
LANES = 128
SUBLANES = 8
HALO = 2 * SUBLANES
CONV_COLS = 512
SEQS_PER_STEP = 8
VMEM_LIMIT = 56 * 1024 * 1024

GDN_HEADS = 8
GDN_DK = 128
GDN_DV = 128
S5_GROUP = 16
S5_STATE = 64
S5_SLAB = 128
M2_HEAD_DIM = 64
M2_STATE = 128
M2_GROUPS = 8


def _cparams(*sem):
    return pltpu.CompilerParams(dimension_semantics=sem, vmem_limit_bytes=VMEM_LIMIT)


def _rms(x, g):
    return x * lax.rsqrt(jnp.mean(x * x, axis=-1, keepdims=True) + RMS_EPS) * g


def _sigmoid(x):
    return 0.5 * (jnp.tanh(0.5 * x) + 1.0)


def _silu(x):
    hx = 0.5 * x
    return hx + hx * jnp.tanh(hx)


def _softplus(x):
    return jnp.maximum(x, 0.0) + jnp.log(1.0 + jnp.exp(-jnp.abs(x)))


def _dot(a, b):
    return jnp.dot(a, b, preferred_element_type=F32)


def _split2(x):
    hi = x.astype(BF16)
    return hi, (x - hi.astype(F32)).astype(BF16)


def _split3(x):
    hi = x.astype(BF16)
    r = x - hi.astype(F32)
    mid = r.astype(BF16)
    return hi, mid, (r - mid.astype(F32)).astype(BF16)


def _dot_nt(a, b):
    return lax.dot_general(a, b, (((1,), (1,)), ((), ())), preferred_element_type=F32)


def _dot_tn(a, b):
    return lax.dot_general(a, b, (((0,), (0,)), ((), ())), preferred_element_type=F32)


def _causal_masks(n):
    row = lax.broadcasted_iota(jnp.int32, (n, n), 0)
    col = lax.broadcasted_iota(jnp.int32, (n, n), 1)
    return row >= col, row > col


def _norm_mm_kernel(x_ref, g_ref, w_ref, *rest, small):
    if small:
        ws_ref, o_ref, os_ref, xn_ref = rest
    else:
        o_ref, xn_ref = rest

    @pl.when(pl.program_id(1) == 0)
    def _():
        xn = _rms(x_ref[...], g_ref[...]).astype(BF16)
        xn_ref[...] = xn
        if small:
            os_ref[...] = _dot(xn, ws_ref[...])

    o_ref[...] = _dot(xn_ref[...], w_ref[...]).astype(o_ref.dtype)


def norm_matmul(h, g, w_stack, layer, n_out, w_small=None, *, out_dtype, tm=1024, tn=2048):
    T, D = h.shape
    N = n_out
    tm = min(tm, T)
    tn = min(tn, N)
    ni, nj = T // tm, N // tn
    x_spec = pl.BlockSpec((tm, D), lambda i, j: (i, 0))
    g_spec = pl.BlockSpec((1, D), lambda i, j: (0, 0))
    w_spec = pl.BlockSpec((None, D, tn), lambda i, j: (layer, 0, j))
    out_shape = jax.ShapeDtypeStruct((T, N), out_dtype)
    o_spec = pl.BlockSpec((tm, tn), lambda i, j: (i, j))
    in_specs = [x_spec, g_spec, w_spec]
    args = [h, g.reshape(1, D), w_stack]
    small = w_small is not None
    if small:
        ns = w_small.shape[1]
        in_specs.append(pl.BlockSpec((D, ns), lambda i, j: (0, 0)))
        args.append(w_small)
        o_spec = [o_spec, pl.BlockSpec((tm, ns), lambda i, j: (i, 0))]
        out_shape = [out_shape, jax.ShapeDtypeStruct((T, ns), F32)]
    return pl.pallas_call(
        functools.partial(_norm_mm_kernel, small=small), grid=(ni, nj), in_specs=in_specs,
        out_specs=o_spec, out_shape=out_shape,
        scratch_shapes=[pltpu.VMEM((tm, D), BF16)],
        compiler_params=_cparams("parallel", "arbitrary"), name="norm_matmul",
    )(*args)


def _out_mlp_kernel(mix_ref, wo_ref, r_ref, g_ref, w1_ref, w2_ref, *rest, glu, final):
    if final:
        gf_ref, o_ref, xn_ref = rest
    else:
        o_ref, xn_ref = rest
    j = pl.program_id(1)

    @pl.when(j == 0)
    def _():
        m = _dot(mix_ref[...], wo_ref[...])
        if glu:
            n = o_ref.shape[1]
            m = m[:, :n] * _sigmoid(m[:, n:])
        h1 = r_ref[...] + m
        o_ref[...] = h1
        xn_ref[...] = _rms(h1, g_ref[...]).astype(BF16)

    a = jnp.maximum(_dot(xn_ref[...], w1_ref[...]), 0.0)
    o_ref[...] += _dot((a * a).astype(BF16), w2_ref[...])

    if final:
        @pl.when(j == pl.num_programs(1) - 1)
        def _():
            o_ref[...] = _rms(o_ref[...], gf_ref[...])


def out_proj_mlp(mix, w_out, res, g, w1_stack, w2_stack, layer, g_final=None, *,
                 glu=False, tm=1024, tf=1024):
    T, K = mix.shape
    D = res.shape[1]
    FF = w1_stack.shape[2]
    tm = min(tm, T)
    final = g_final is not None
    in_specs = [pl.BlockSpec((tm, K), lambda i, j: (i, 0)),
                pl.BlockSpec(w_out.shape, lambda i, j: (0, 0)),
                pl.BlockSpec((tm, D), lambda i, j: (i, 0)),
                pl.BlockSpec((1, D), lambda i, j: (0, 0)),
                pl.BlockSpec((None, D, tf), lambda i, j: (layer, 0, j)),
                pl.BlockSpec((None, tf, D), lambda i, j: (layer, j, 0))]
    args = [mix, w_out, res, g.reshape(1, D), w1_stack, w2_stack]
    if final:
        in_specs.append(pl.BlockSpec((1, D), lambda i, j: (0, 0)))
        args.append(g_final.reshape(1, D))
    return pl.pallas_call(
        functools.partial(_out_mlp_kernel, glu=glu, final=final), grid=(T // tm, FF // tf),
        in_specs=in_specs,
        out_specs=pl.BlockSpec((tm, D), lambda i, j: (i, 0)),
        out_shape=jax.ShapeDtypeStruct((T, D), F32),
        scratch_shapes=[pltpu.VMEM((tm, D), BF16)],
        compiler_params=_cparams("parallel", "arbitrary"), name="out_proj_mlp",
    )(*args)


def _conv_window(cur, prev, xb_ref, first_chunk):
    xb_ref[0:HALO, :] = jnp.where(first_chunk, jnp.zeros_like(prev), prev)
    xb_ref[HALO:HALO + CHUNK, :] = cur


def _causal_conv(xb_ref, cw_ref):
    window = HALO + CHUNK
    i = lax.broadcasted_iota(jnp.int32, (CHUNK, window), 0)
    c = lax.broadcasted_iota(jnp.int32, (CHUNK, window), 1)
    pick = jnp.concatenate(
        [jnp.where(c == i + (HALO - (CONV_W - 1) + w), 1.0, 0.0) for w in range(CONV_W - 1)],
        axis=1).astype(BF16)
    outs = []
    for c0 in range(0, xb_ref.shape[1], CONV_COLS):
        cols = slice(c0, c0 + CONV_COLS)
        xw = xb_ref[:, cols]
        cw = cw_ref[:, cols]
        cwb = cw.astype(BF16)
        scaled = jnp.concatenate([xw * cwb[w:w + 1] for w in range(CONV_W - 1)], axis=0)
        outs.append(_dot(pick, scaled)
                    + xw[HALO:].astype(F32) * cw[CONV_W - 1:CONV_W])
    return jnp.concatenate(outs, axis=1)


def _chunk_cumsum(x, causal):
    n = x.shape[1]
    parts = _dot(jnp.where(causal, 1.0, 0.0).astype(BF16), jnp.concatenate(_split3(x), axis=1))
    return parts[:, :n] + (parts[:, n:2 * n] + parts[:, 2 * n:])


def _prev_rows_spec(width):
    blocks_per_chunk = CHUNK // HALO
    return pl.BlockSpec(
        (SEQS_PER_STEP, HALO, width),
        lambda b, c: (b, jnp.maximum(c * blocks_per_chunk - 1, 0), 0))


def _dot_split(a, b):
    return _dot(jnp.concatenate([a[0], a[0], a[1]], axis=1),
                jnp.concatenate([b[0], b[1], b[0]], axis=0))


def _unit_lower_inverses(mats, nilpotent):
    n = mats[0].shape[0]
    row = lax.broadcasted_iota(jnp.int32, (n, n), 0)
    col = lax.broadcasted_iota(jnp.int32, (n, n), 1)
    eye = jnp.where(row == col, 1.0, 0.0)
    ps = [eye - a for a in mats]
    aks = [a.astype(BF16) for a in mats]
    power = 2
    while power < nilpotent:
        aks = [_dot(ak, ak).astype(BF16) for ak in aks]
        ps = [p + _dot(p.astype(BF16), ak) for p, ak in zip(ps, aks)]
        power *= 2
    resid = [eye - _dot_split(_split2(eye + a), _split2(p)) for a, p in zip(mats, ps)]
    return [p + _dot(p.astype(BF16), r.astype(BF16)) for p, r in zip(ps, resid)]


def _gdn_kernel(cur_ref, prev_ref, ab_ref, cw_ref, alog_ref, dtb_ref, ong_ref,
                o_ref, xb_ref, s_ref):
    c = pl.program_id(1)
    nseq = cur_ref.shape[0]
    nqk = GDN_HEADS * GDN_DK
    nconv = 2 * nqk + GDN_HEADS * GDN_DV
    units = [(b, p) for b in range(nseq) for p in range(GDN_HEADS // 2)]
    nu = range(len(units))
    stack = lambda f, p: jnp.concatenate([f(2 * p), f(2 * p + 1)], axis=0)

    @pl.when(c == 0)
    def _():
        s_ref[...] = jnp.zeros_like(s_ref)

    for b in range(nseq):
        _conv_window(cur_ref[b, :, :nconv], prev_ref[b, :, :nconv], xb_ref.at[b], c == 0)
    qkv = [_silu(_causal_conv(xb_ref.at[b], cw_ref)) for b in range(nseq)]

    causal, _ = _causal_masks(CHUNK)
    row = lax.broadcasted_iota(jnp.int32, (2 * CHUNK, 2 * CHUNK), 0)
    lane = lax.broadcasted_iota(jnp.int32, (2 * CHUNK, 2 * CHUNK), 1)
    first = lane < CHUNK
    same_head = (row < CHUNK) == first
    t_row = jnp.where(row < CHUNK, row, row - CHUNK)
    t_col = jnp.where(first, lane, lane - CHUNK)
    causal2 = same_head & (t_row >= t_col)
    strict2 = same_head & (t_row > t_col)

    ab = [ab_ref[b] for b in range(nseq)]
    beta_all = [_sigmoid(t) for t in ab]
    G = [_chunk_cumsum(-jnp.exp(alog_ref[...]) * _softplus(t + dtb_ref[...]), causal) for t in ab]
    G_rows = [jnp.concatenate([t, t], axis=0).T for t in G]
    eG = [jnp.exp(t) for t in G]
    eG_rev = [jnp.exp(t[CHUNK - 1:CHUNK, :] - t) for t in G]

    def heads_of(width, offset, b, p):
        return stack(lambda h: qkv[b][:, offset + h * width:offset + (h + 1) * width], p)

    q = [heads_of(GDN_DK, 0, b, p) for b, p in units]
    k = [heads_of(GDN_DK, nqk, b, p) for b, p in units]
    v = [heads_of(GDN_DV, 2 * nqk, b, p) for b, p in units]
    squares = jnp.concatenate([t * t for t in q + k], axis=0).astype(BF16)
    inv_norm = lax.rsqrt(_dot(squares, jnp.ones((GDN_DK, GDN_DK), BF16)) + L2_EPS)
    rows = 2 * CHUNK
    q = [q[u] * inv_norm[u * rows:(u + 1) * rows] * (GDN_DK ** -0.5) for u in nu]
    k = [k[u] * inv_norm[(len(units) + u) * rows:(len(units) + u + 1) * rows] for u in nu]
    col_of = lambda a, b, p, off=0: jnp.broadcast_to(
        stack(lambda h: a[b][:, off + h:off + h + 1], p), (2 * CHUNK, LANES))
    beta = [col_of(beta_all, b, p, GDN_HEADS) for b, p in units]
    eg = [col_of(eG, b, p) for b, p in units]
    decay = []
    for b, p in units:
        g_row = jnp.where(first[0:1], G_rows[b][2 * p:2 * p + 1, :], G_rows[b][2 * p + 1:2 * p + 2, :])
        diff = col_of(G, b, p) - g_row
        decay.append(jnp.where(causal2, jnp.exp(jnp.where(causal2, diff, 0.0)), 0.0))
    kb = [t.astype(BF16) for t in k]
    qkk = [_dot_nt(jnp.concatenate([q[u].astype(BF16), kb[u]], axis=0), kb[u]) for u in nu]
    t_inv = _unit_lower_inverses(
        [jnp.where(strict2, beta[u] * qkk[u][rows:] * decay[u], 0.0) for u in nu], CHUNK)
    uw = [_dot(t_inv[u].astype(BF16),
               jnp.concatenate([(v[u] * beta[u]).astype(BF16),
                                (k[u] * (beta[u] * eg[u])).astype(BF16)], axis=1))
          for u in nu]
    qe = [q[u] * eg[u] for u in nu]
    halves = (slice(0, CHUNK), slice(CHUNK, rows))
    S = [[s_ref[b, 2 * p + i] for i in range(2)] for b, p in units]
    Sb = [[t.astype(BF16) for t in pair] for pair in S]
    ws = [[_dot(jnp.concatenate([uw[u][halves[i], GDN_DV:], qe[u][halves[i]]], axis=0).astype(BF16),
                Sb[u][i]) for i in range(2)] for u in nu]
    vb = [(uw[u][:, :GDN_DV] - jnp.concatenate([ws[u][0][:CHUNK], ws[u][1][:CHUNK]], axis=0))
          .astype(BF16) for u in nu]
    o = [jnp.concatenate([ws[u][0][CHUNK:], ws[u][1][CHUNK:]], axis=0)
         + _dot((qkk[u][:rows] * decay[u]).astype(BF16), vb[u]) for u in nu]
    k_dec = [(k[u] * col_of(eG_rev, b, p)).astype(BF16) for u, (b, p) in enumerate(units)]
    s_new = [[eG[b][CHUNK - 1:CHUNK, 2 * p + i:2 * p + i + 1] * S[u][i]
              + _dot_tn(k_dec[u][halves[i]], vb[u][halves[i]]) for i in range(2)]
             for u, (b, p) in enumerate(units)]
    for u, (b, p) in enumerate(units):
        for i in range(2):
            s_ref[b, 2 * p + i] = s_new[u][i]
    for u, (b, p) in enumerate(units):
        for i in range(2):
            h = 2 * p + i
            gate = cur_ref[b, :, nconv + h * GDN_DV:nconv + (h + 1) * GDN_DV].astype(F32)
            o_ref[b, :, h * GDN_DV:(h + 1) * GDN_DV] = (
                _rms(o[u][halves[i]], ong_ref[...]) * _silu(gate)).astype(o_ref.dtype)


def gdn_core(proj, ab, conv_w, a_log, dt_bias, o_norm_g):
    batch, seq_len, W = proj.shape
    nv = GDN_HEADS * GDN_DV
    nconv = conv_w.shape[1]
    nb = SEQS_PER_STEP
    pad = lambda a: jnp.pad(a.reshape(1, -1), ((0, 0), (0, LANES - a.shape[-1])))
    const = lambda shape: pl.BlockSpec(shape, lambda b, c: (0, 0))
    return pl.pallas_call(
        _gdn_kernel, grid=(batch // nb, seq_len // CHUNK),
        in_specs=[pl.BlockSpec((nb, CHUNK, W), lambda b, c: (b, c, 0)),
                  _prev_rows_spec(W),
                  pl.BlockSpec((nb, CHUNK, LANES), lambda b, c: (b, c, 0)),
                  const((CONV_W, nconv)), const((1, LANES)), const((1, LANES)),
                  const((1, GDN_DV))],
        out_specs=pl.BlockSpec((nb, CHUNK, nv), lambda b, c: (b, c, 0)),
        out_shape=jax.ShapeDtypeStruct((batch, seq_len, nv), BF16),
        scratch_shapes=[pltpu.VMEM((nb, HALO + CHUNK, nconv), BF16),
                        pltpu.VMEM((nb, GDN_HEADS, GDN_DK, GDN_DV), F32)],
        compiler_params=_cparams("parallel", "arbitrary"), name="gdn_core",
    )(proj, proj, ab, conv_w, pad(a_log), pad(dt_bias), o_norm_g.reshape(1, GDN_DV))


def gdn_mixer(h, g_mix, w_in_stack, layer, conv_w, a_log, dt_bias, o_norm_g, *, batch, seq_len):
    nmain = conv_w.shape[1] + GDN_HEADS * GDN_DV
    w_small = jnp.pad(w_in_stack[layer][:, nmain:], ((0, 0), (0, LANES - 2 * GDN_HEADS)))
    proj, ab = norm_matmul(h, g_mix, w_in_stack, layer, nmain, w_small, out_dtype=BF16)
    o = gdn_core(proj.reshape(batch, seq_len, -1), ab.reshape(batch, seq_len, -1),
                 conv_w, a_log, dt_bias, o_norm_g)
    return o.reshape(batch * seq_len, -1)


def _m2_kernel(cur_ref, prev_ref, dt_ref, cw_ref, cb_ref, dtb_ref, alog_ref, dsk_ref,
               ng_ref, sel_ref, o_ref, xb_ref, s_ref):
    c = pl.program_id(1)
    nseq = cur_ref.shape[0]
    seqs = range(nseq)
    inner = dsk_ref.shape[1]
    nbc = M2_GROUPS * M2_STATE
    pair_w = 2 * M2_HEAD_DIM
    pairs_per_group = inner // M2_GROUPS // pair_w
    gw = inner // M2_GROUPS

    @pl.when(c == 0)
    def _():
        s_ref[...] = jnp.zeros_like(s_ref)

    for b in seqs:
        _conv_window(cur_ref[b, :, inner:], prev_ref[b, :, inner:], xb_ref.at[b], c == 0)
    xbc = [_silu(_causal_conv(xb_ref.at[b], cw_ref) + cb_ref[...]) for b in seqs]
    x = [t[:, :inner] for t in xbc]

    causal, _ = _causal_masks(CHUNK)
    dt = [_softplus(dt_ref[b] + dtb_ref[...]) for b in seqs]
    cum = [_chunk_cumsum(t * (-jnp.exp(alog_ref[...])), causal) for t in dt]
    cum_rows = [jnp.concatenate([t, t], axis=0).T for t in cum]
    cum_hi, cum_lo = zip(*[_split2(t) for t in cum])
    scales = [a.astype(BF16) for b in seqs
              for a in (dt[b], jnp.exp(cum[b]), dt[b] * jnp.exp(cum[b][CHUNK - 1:CHUNK, :] - cum[b]))]
    spread = _dot(jnp.concatenate(list(cum_hi) + list(cum_lo) + scales, axis=0), sel_ref[...])
    block = lambda n: spread[n * CHUNK:(n + 1) * CHUNK]
    cum_x = [block(b) + block(nseq + b) for b in seqs]
    xdt = [x[b] * block(2 * nseq + 3 * b) for b in seqs]
    ecum = [block(2 * nseq + 3 * b + 1) for b in seqs]
    xd = [x[b] * block(2 * nseq + 3 * b + 2) for b in seqs]

    lane = lax.broadcasted_iota(jnp.int32, (CHUNK, pair_w), 1)
    row = lax.broadcasted_iota(jnp.int32, (CHUNK, pair_w), 0)
    first = lane < M2_HEAD_DIM
    causal2 = row >= jnp.where(first, lane, lane - M2_HEAD_DIM)

    units = [(b, g) for b in seqs for g in range(M2_GROUPS)]
    gcol = lambda g: slice(g * gw, (g + 1) * gw)
    Bg = [xbc[b][:, inner + g * M2_STATE:inner + (g + 1) * M2_STATE].astype(BF16) for b, g in units]
    Cg = [xbc[b][:, inner + nbc + g * M2_STATE:inner + nbc + (g + 1) * M2_STATE].astype(BF16)
          for b, g in units]
    cb2 = [_dot_nt(Cg[u], jnp.concatenate([Bg[u], Bg[u]], axis=0))
           for u in range(len(units))]
    S = [s_ref[b, g] for b, g in units]
    y_off = [_dot(Cg[u], S[u].astype(BF16)) * ecum[b][:, gcol(g)] for u, (b, g) in enumerate(units)]
    s_new = [ecum[b][CHUNK - 1:CHUNK, gcol(g)] * S[u]
             + _dot_tn(Bg[u], xd[b][:, gcol(g)].astype(BF16)) for u, (b, g) in enumerate(units)]
    for u, (b, g) in enumerate(units):
        s_ref[b, g] = s_new[u]
    y_diag = []
    for u, (b, g) in enumerate(units):
        for m in range(pairs_per_group):
            p = g * pairs_per_group + m
            cols = slice(p * pair_w, (p + 1) * pair_w)
            h1 = 2 * p
            row_pair = jnp.where(first[0:1], cum_rows[b][h1:h1 + 1, :], cum_rows[b][h1 + 1:h1 + 2, :])
            diff = cum_x[b][:, cols] - row_pair
            lmat = jnp.where(causal2, jnp.exp(jnp.where(causal2, diff, 0.0)), 0.0)
            xp = xdt[b][:, cols]
            x_bd = jnp.concatenate([jnp.where(first, xp, 0.0), jnp.where(first, 0.0, xp)], axis=0)
            y_diag.append(_dot((cb2[u] * lmat).astype(BF16), x_bd.astype(BF16)))
    for u, (b, g) in enumerate(units):
        cols = gcol(g)
        yd = jnp.concatenate(y_diag[u * pairs_per_group:(u + 1) * pairs_per_group], axis=1)
        y = ((yd + y_off[u] + dsk_ref[:, cols] * x[b][:, cols])
             * _silu(cur_ref[b, :, cols].astype(F32)))
        o_ref[b, :, cols] = _rms(y, ng_ref[:, cols]).astype(o_ref.dtype)


def m2_core(proj, dt_raw, conv_w, conv_b, dt_bias, a_log, d_skip, norm_g):
    batch, seq_len, W = proj.shape
    nconv = conv_w.shape[1]
    inner = W - nconv
    nb = SEQS_PER_STEP
    pad = lambda a: jnp.pad(a.reshape(1, -1), ((0, 0), (0, LANES - a.shape[-1])))
    head_of_lane = jnp.arange(inner, dtype=jnp.int32) // M2_HEAD_DIM
    sel = (jnp.arange(LANES, dtype=jnp.int32)[:, None] == head_of_lane[None, :]).astype(BF16)
    d_x = jnp.repeat(d_skip, M2_HEAD_DIM).reshape(1, inner)
    const = lambda shape: pl.BlockSpec(shape, lambda b, c: (0, 0))
    return pl.pallas_call(
        _m2_kernel, grid=(batch // nb, seq_len // CHUNK),
        in_specs=[pl.BlockSpec((nb, CHUNK, W), lambda b, c: (b, c, 0)),
                  _prev_rows_spec(W),
                  pl.BlockSpec((nb, CHUNK, LANES), lambda b, c: (b, c, 0)),
                  const((CONV_W, nconv)), const((1, nconv)), const((1, LANES)),
                  const((1, LANES)), const((1, inner)), const((1, inner)),
                  const((LANES, inner))],
        out_specs=pl.BlockSpec((nb, CHUNK, inner), lambda b, c: (b, c, 0)),
        out_shape=jax.ShapeDtypeStruct((batch, seq_len, inner), BF16),
        scratch_shapes=[pltpu.VMEM((nb, HALO + CHUNK, nconv), BF16),
                        pltpu.VMEM((nb, M2_GROUPS, M2_STATE, inner // M2_GROUPS), F32)],
        compiler_params=_cparams("parallel", "arbitrary"), name="m2_core",
    )(proj, proj, dt_raw, conv_w, conv_b.reshape(1, nconv), pad(dt_bias), pad(a_log),
      d_x, norm_g.reshape(1, inner), sel)


def m2_mixer(h, g_mix, w_in_stack, layer, conv_w, conv_b, dt_bias, a_log, d_skip, norm_g,
             *, batch, seq_len):
    heads = dt_bias.shape[0]
    nmain = w_in_stack.shape[2] - heads
    w_small = jnp.pad(w_in_stack[layer][:, nmain:], ((0, 0), (0, LANES - heads)))
    proj, dt_raw = norm_matmul(h, g_mix, w_in_stack, layer, nmain, w_small, out_dtype=BF16)
    y = m2_core(proj.reshape(batch, seq_len, -1), dt_raw.reshape(batch, seq_len, -1),
                conv_w, conv_b, dt_bias, a_log, d_skip, norm_g)
    return y.reshape(batch * seq_len, -1)


def _s5_prep_kernel(lr_ref, li_ref, ldt_ref, br_ref, bi_ref, lbr_ref, lbi_ref, bbr_ref, bbi_ref):
    lr, li = lr_ref[...], li_ref[...]
    dt = jnp.exp(ldt_ref[...])
    mag = jnp.exp(lr * dt)
    lbr = mag * jnp.cos(li * dt)
    lbi = mag * jnp.sin(li * dt)
    nr = lbr - 1.0
    den = lr * lr + li * li
    qr = (nr * lr + lbi * li) / den
    qi = (lbi * lr - nr * li) / den
    br, bi = br_ref[...], bi_ref[...]
    lbr_ref[...] = lbr
    lbi_ref[...] = lbi
    bbr_ref[...] = qr * br - qi * bi
    bbi_ref[...] = qr * bi + qi * br


def _s5_kernel(u_ref, b_ref, cre_ref, cim_ref, lam_ref, d_ref, y_ref, bu_ref, st_ref, *, steps):
    nslab = b_ref.shape[0]
    half = b_ref.shape[2] // 2

    @pl.when(pl.program_id(0) == 0)
    def _():
        st_ref[...] = jnp.zeros_like(st_ref)

    batch, _, d_model = u_ref.shape
    rows = batch * steps

    dst = lax.broadcasted_iota(jnp.int32, (rows, rows), 0)
    src = lax.broadcasted_iota(jnp.int32, (rows, rows), 1)
    perm = jnp.where((dst // batch == src % steps) & (dst % batch == src // steps),
                     1.0, 0.0).astype(BF16)
    u_hi, u_lo = _split2(u_ref[...].reshape(rows, d_model))
    u_hi = _dot(perm, u_hi)
    u_tm = u_hi + _dot(perm, u_lo)
    ub = u_hi.astype(BF16)

    for s in range(nslab):
        bu_ref[:, 2 * half * s:2 * half * (s + 1)] = _dot(
            ub[:, s * S5_SLAB:(s + 1) * S5_SLAB], b_ref[s])

    for s in range(nslab):
        re = slice(2 * half * s, 2 * half * s + half)
        im = slice(2 * half * s + half, 2 * half * (s + 1))
        lam_r = jnp.broadcast_to(lam_ref[:, re], (batch, half))
        lam_i = jnp.broadcast_to(lam_ref[:, im], (batch, half))

        def step(t, carry, re=re, im=im, lam_r=lam_r, lam_i=lam_i):
            xr, xi = carry
            at_t = pl.ds(pl.multiple_of(t * batch, batch), batch)
            nr = lam_r * xr - lam_i * xi + bu_ref[at_t, re]
            ni = lam_r * xi + lam_i * xr + bu_ref[at_t, im]
            bu_ref[at_t, re] = nr
            bu_ref[at_t, im] = ni
            return nr, ni

        xr, xi = lax.fori_loop(0, steps, step, (st_ref[:, re], st_ref[:, im]))
        st_ref[:, re] = xr
        st_ref[:, im] = xi

    ys = []
    for s in range(nslab):
        re = slice(2 * half * s, 2 * half * s + half)
        im = slice(2 * half * s + half, 2 * half * (s + 1))
        cols = slice(s * S5_SLAB, (s + 1) * S5_SLAB)
        y = (_dot(bu_ref[:, re].astype(BF16), cre_ref[s])
             - _dot(bu_ref[:, im].astype(BF16), cim_ref[s])
             + d_ref[:, cols] * u_tm[:, cols])
        ys.append(jax.nn.gelu(y, approximate=True).astype(BF16))
    y_sm = _dot_tn(perm, jnp.concatenate(ys, axis=1))
    y_ref[...] = y_sm.astype(y_ref.dtype).reshape(batch, steps, d_model)


def s5_core(u, lam_re, lam_im, log_dt, b_re, b_im, c_re, c_im, d_skip, *, steps=32):
    batch, L, D = u.shape
    assert batch == SUBLANES, "one time step's batch entries fill one sublane tile"
    steps = min(steps, L)
    G, P = lam_re.shape
    K = S5_GROUP
    nslab = D // S5_SLAB
    gps = S5_SLAB // K
    half = gps * P

    rep = lambda a: jnp.repeat(a, K, axis=-1)
    flat = jax.ShapeDtypeStruct((G, P * K), F32)
    lbr, lbi, bbr, bbi = pl.pallas_call(
        _s5_prep_kernel, out_shape=[flat] * 4, name="s5_prep",
    )(rep(lam_re), rep(lam_im), jnp.broadcast_to(log_dt[:, None], (G, P * K)),
      b_re.reshape(G, P * K), b_im.reshape(G, P * K))

    eye = jnp.eye(gps, dtype=F32)
    to_slab = lambda a: a.reshape(G, P, K).transpose(0, 2, 1).reshape(nslab, gps, K, P)
    b_blk = jnp.concatenate(
        [jnp.einsum('sjkp,jJ->sjkJp', to_slab(a), eye).reshape(nslab, S5_SLAB, half)
         for a in (bbr, bbi)], axis=-1).astype(BF16)
    c_blk = [jnp.einsum('sjkp,jJ->sJpjk', a.reshape(nslab, gps, K, P), eye)
             .reshape(nslab, half, S5_SLAB).astype(BF16) for a in (c_re, c_im)]
    lam = jnp.stack([lbr[:, ::K].reshape(nslab, half), lbi[:, ::K].reshape(nslab, half)],
                    axis=1).reshape(1, nslab * 2 * half)

    const3 = lambda shape: pl.BlockSpec(shape, lambda i: (0, 0, 0))
    return pl.pallas_call(
        functools.partial(_s5_kernel, steps=steps), grid=(L // steps,),
        in_specs=[pl.BlockSpec((batch, steps, D), lambda i: (0, i, 0)),
                  const3((nslab, S5_SLAB, 2 * half)),
                  const3((nslab, half, S5_SLAB)), const3((nslab, half, S5_SLAB)),
                  pl.BlockSpec((1, nslab * 2 * half), lambda i: (0, 0)),
                  pl.BlockSpec((1, D), lambda i: (0, 0))],
        out_specs=pl.BlockSpec((batch, steps, D), lambda i: (0, i, 0)),
        out_shape=jax.ShapeDtypeStruct((batch, L, D), BF16),
        scratch_shapes=[pltpu.VMEM((batch * steps, nslab * 2 * half), F32),
                        pltpu.VMEM((batch, nslab * 2 * half), F32)],
        compiler_params=_cparams("arbitrary"), name="s5_core",
    )(u, b_blk, c_blk[0], c_blk[1], lam, d_skip.reshape(1, D))


def s5_mixer(h, g_mix, w_in_stack, layer, lam_re, lam_im, log_dt, b_re, b_im, c_re, c_im, d_skip,
             *, batch, seq_len):
    T, D = h.shape
    u = norm_matmul(h, g_mix, w_in_stack, layer, D, out_dtype=F32)
    y = s5_core(u.reshape(batch, seq_len, D), lam_re, lam_im, log_dt, b_re, b_im, c_re, c_im,
                d_skip)
    return y.reshape(T, D)


def kernel(x, norm_mix_g, norm_mlp_g, mlp_w1, mlp_w2, gdn_w_in, gdn_conv_w, gdn_a_log, gdn_dt_bias, gdn_o_norm_g, gdn_w_out, s5_w_in, s5_lam_re, s5_lam_im, s5_log_dt, s5_b_re, s5_b_im, s5_c_re, s5_c_im, s5_d, s5_w_out, m2_w_in, m2_conv_w, m2_conv_b, m2_dt_bias, m2_a_log, m2_d, m2_norm_g, m2_w_out, final_norm_g):
    batch, seq_len, d_model = x.shape
    depth = norm_mix_g.shape[0]
    h = x.reshape(batch * seq_len, d_model)
    mlp_w1, mlp_w2, gdn_w_in, s5_w_in, m2_w_in = (
        w.astype(BF16) for w in (mlp_w1, mlp_w2, gdn_w_in, s5_w_in, m2_w_in))
    for i in range(depth):
        kind, j = i % 3, i // 3
        if kind == 0:
            mix = gdn_mixer(h, norm_mix_g[i], gdn_w_in, j, gdn_conv_w[j], gdn_a_log[j],
                            gdn_dt_bias[j], gdn_o_norm_g[j], batch=batch, seq_len=seq_len)
            w_out = gdn_w_out[j]
        elif kind == 1:
            mix = s5_mixer(h, norm_mix_g[i], s5_w_in, j, s5_lam_re[j], s5_lam_im[j], s5_log_dt[j],
                           s5_b_re[j], s5_b_im[j], s5_c_re[j], s5_c_im[j], s5_d[j],
                           batch=batch, seq_len=seq_len)
            w_out = s5_w_out[j]
        else:
            mix = m2_mixer(h, norm_mix_g[i], m2_w_in, j, m2_conv_w[j], m2_conv_b[j], m2_dt_bias[j],
                           m2_a_log[j], m2_d[j], m2_norm_g[j], batch=batch, seq_len=seq_len)
            w_out = m2_w_out[j]
        h = out_proj_mlp(mix, w_out.astype(BF16), h, norm_mlp_g[i], mlp_w1, mlp_w2, i,
                         final_norm_g if i == depth - 1 else None, glu=(kind == 1))
    return h.reshape(batch, seq_len, d_model)
```

```python
import functools

import jax
import jax.numpy as jnp
from jax import lax
from jax.experimental import pallas as pl
from jax.experimental.pallas import tpu as pltpu

F32 = jnp.float32
BF16 = jnp.bfloat16

RMS_EPS = 1e-6
L2_EPS = 1e-6
CHUNK = 64
CONV_W = 4
LANES = 128
SUBLANES = 8
HALO = 2 * SUBLANES
CONV_COLS = 512
SEQS_PER_STEP = 8
VMEM_LIMIT = 56 * 1024 * 1024

GDN_HEADS = 8
GDN_DK = 128
GDN_DV = 128
S5_GROUP = 16
S5_STATE = 64
S5_SLAB = 128
M2_HEAD_DIM = 64
M2_STATE = 128
M2_GROUPS = 8


def _cparams(*sem):
    return pltpu.CompilerParams(dimension_semantics=sem, vmem_limit_bytes=VMEM_LIMIT)


def _rms(x, g):
    return x * lax.rsqrt(jnp.mean(x * x, axis=-1, keepdims=True) + RMS_EPS) * g


def _sigmoid(x):
    return 0.5 * (jnp.tanh(0.5 * x) + 1.0)


def _silu(x):
    hx = 0.5 * x
    return hx + hx * jnp.tanh(hx)


def _softplus(x):
    return jnp.maximum(x, 0.0) + jnp.log(1.0 + jnp.exp(-jnp.abs(x)))


def _dot(a, b):
    return jnp.dot(a, b, preferred_element_type=F32)


def _split2(x):
    hi = x.astype(BF16)
    return hi, (x - hi.astype(F32)).astype(BF16)


def _split3(x):
    hi = x.astype(BF16)
    r = x - hi.astype(F32)
    mid = r.astype(BF16)
    return hi, mid, (r - mid.astype(F32)).astype(BF16)


def _dot_nt(a, b):
    return lax.dot_general(a, b, (((1,), (1,)), ((), ())), preferred_element_type=F32)


def _dot_tn(a, b):
    return lax.dot_general(a, b, (((0,), (0,)), ((), ())), preferred_element_type=F32)


def _causal_masks(n):
    row = lax.broadcasted_iota(jnp.int32, (n, n), 0)
    col = lax.broadcasted_iota(jnp.int32, (n, n), 1)
    return row >= col, row > col


def _norm_mm_kernel(x_ref, g_ref, w_ref, *rest, small):
    if small:
        ws_ref, o_ref, os_ref, xn_ref = rest
    else:
        o_ref, xn_ref = rest

    @pl.when(pl.program_id(1) == 0)
    def _():
        xn = _rms(x_ref[...], g_ref[...]).astype(BF16)
        xn_ref[...] = xn
        if small:
            os_ref[...] = _dot(xn, ws_ref[...])

    o_ref[...] = _dot(xn_ref[...], w_ref[...]).astype(o_ref.dtype)


def norm_matmul(h, g, w_stack, layer, n_out, w_small=None, *, out_dtype, tm=1024, tn=2048):
    T, D = h.shape
    N = n_out
    tm = min(tm, T)
    tn = min(tn, N)
    ni, nj = T // tm, N // tn
    x_spec = pl.BlockSpec((tm, D), lambda i, j: (i, 0))
    g_spec = pl.BlockSpec((1, D), lambda i, j: (0, 0))
    w_spec = pl.BlockSpec((None, D, tn), lambda i, j: (layer, 0, j))
    out_shape = jax.ShapeDtypeStruct((T, N), out_dtype)
    o_spec = pl.BlockSpec((tm, tn), lambda i, j: (i, j))
    in_specs = [x_spec, g_spec, w_spec]
    args = [h, g.reshape(1, D), w_stack]
    small = w_small is not None
    if small:
        ns = w_small.shape[1]
        in_specs.append(pl.BlockSpec((D, ns), lambda i, j: (0, 0)))
        args.append(w_small)
        o_spec = [o_spec, pl.BlockSpec((tm, ns), lambda i, j: (i, 0))]
        out_shape = [out_shape, jax.ShapeDtypeStruct((T, ns), F32)]
    return pl.pallas_call(
        functools.partial(_norm_mm_kernel, small=small), grid=(ni, nj), in_specs=in_specs,
        out_specs=o_spec, out_shape=out_shape,
        scratch_shapes=[pltpu.VMEM((tm, D), BF16)],
        compiler_params=_cparams("parallel", "arbitrary"), name="norm_matmul",
    )(*args)


def _out_mlp_kernel(mix_ref, wo_ref, r_ref, g_ref, w1_ref, w2_ref, *rest, glu, final):
    if final:
        gf_ref, o_ref, xn_ref = rest
    else:
        o_ref, xn_ref = rest
    j = pl.program_id(1)

    @pl.when(j == 0)
    def _():
        m = _dot(mix_ref[...], wo_ref[...])
        if glu:
            n = o_ref.shape[1]
            m = m[:, :n] * _sigmoid(m[:, n:])
        h1 = r_ref[...] + m
        o_ref[...] = h1
        xn_ref[...] = _rms(h1, g_ref[...]).astype(BF16)

    a = jnp.maximum(_dot(xn_ref[...], w1_ref[...]), 0.0)
    o_ref[...] += _dot((a * a).astype(BF16), w2_ref[...])

    if final:
        @pl.when(j == pl.num_programs(1) - 1)
        def _():
            o_ref[...] = _rms(o_ref[...], gf_ref[...])


def out_proj_mlp(mix, w_out, res, g, w1_stack, w2_stack, layer, g_final=None, *,
                 glu=False, tm=1024, tf=1024):
    T, K = mix.shape
    D = res.shape[1]
    FF = w1_stack.shape[2]
    tm = min(tm, T)
    final = g_final is not None
    in_specs = [pl.BlockSpec((tm, K), lambda i, j: (i, 0)),
                pl.BlockSpec(w_out.shape, lambda i, j: (0, 0)),
                pl.BlockSpec((tm, D), lambda i, j: (i, 0)),
                pl.BlockSpec((1, D), lambda i, j: (0, 0)),
                pl.BlockSpec((None, D, tf), lambda i, j: (layer, 0, j)),
                pl.BlockSpec((None, tf, D), lambda i, j: (layer, j, 0))]
    args = [mix, w_out, res, g.reshape(1, D), w1_stack, w2_stack]
    if final:
        in_specs.append(pl.BlockSpec((1, D), lambda i, j: (0, 0)))
        args.append(g_final.reshape(1, D))
    return pl.pallas_call(
        functools.partial(_out_mlp_kernel, glu=glu, final=final), grid=(T // tm, FF // tf),
        in_specs=in_specs,
        out_specs=pl.BlockSpec((tm, D), lambda i, j: (i, 0)),
        out_shape=jax.ShapeDtypeStruct((T, D), F32),
        scratch_shapes=[pltpu.VMEM((tm, D), BF16)],
        compiler_params=_cparams("parallel", "arbitrary"), name="out_proj_mlp",
    )(*args)


def _conv_window(cur, prev, xb_ref, first_chunk):
    xb_ref[0:HALO, :] = jnp.where(first_chunk, jnp.zeros_like(prev), prev)
    xb_ref[HALO:HALO + CHUNK, :] = cur


def _causal_conv(xb_ref, cw_ref):
    window = HALO + CHUNK
    i = lax.broadcasted_iota(jnp.int32, (CHUNK, window), 0)
    c = lax.broadcasted_iota(jnp.int32, (CHUNK, window), 1)
    pick = jnp.concatenate(
        [jnp.where(c == i + (HALO - (CONV_W - 1) + w), 1.0, 0.0) for w in range(CONV_W - 1)],
        axis=1).astype(BF16)
    outs = []
    for c0 in range(0, xb_ref.shape[1], CONV_COLS):
        cols = slice(c0, c0 + CONV_COLS)
        xw = xb_ref[:, cols]
        cw = cw_ref[:, cols]
        cwb = cw.astype(BF16)
        scaled = jnp.concatenate([xw * cwb[w:w + 1] for w in range(CONV_W - 1)], axis=0)
        outs.append(_dot(pick, scaled)
                    + xw[HALO:].astype(F32) * cw[CONV_W - 1:CONV_W])
    return jnp.concatenate(outs, axis=1)


def _chunk_cumsum(x, causal):
    n = x.shape[1]
    parts = _dot(jnp.where(causal, 1.0, 0.0).astype(BF16), jnp.concatenate(_split3(x), axis=1))
    return parts[:, :n] + (parts[:, n:2 * n] + parts[:, 2 * n:])


def _prev_rows_spec(width):
    blocks_per_chunk = CHUNK // HALO
    return pl.BlockSpec(
        (SEQS_PER_STEP, HALO, width),
        lambda b, c: (b, jnp.maximum(c * blocks_per_chunk - 1, 0), 0))


def _dot_split(a, b):
    return _dot(jnp.concatenate([a[0], a[0], a[1]], axis=1),
                jnp.concatenate([b[0], b[1], b[0]], axis=0))


def _unit_lower_inverses(mats, nilpotent):
    n = mats[0].shape[0]
    row = lax.broadcasted_iota(jnp.int32, (n, n), 0)
    col = lax.broadcasted_iota(jnp.int32, (n, n), 1)
    eye = jnp.where(row == col, 1.0, 0.0)
    ps = [eye - a for a in mats]
    aks = [a.astype(BF16) for a in mats]
    power = 2
    while 2 * power < nilpotent:
        aks = [_dot(ak, ak).astype(BF16) for ak in aks]
        ps = [p + _dot(p.astype(BF16), ak) for p, ak in zip(ps, aks)]
        power *= 2
    resid = [eye - _dot_split(_split2(eye + a), _split2(p)) for a, p in zip(mats, ps)]
    return [p + _dot(p.astype(BF16), r.astype(BF16)) for p, r in zip(ps, resid)]


def _gdn_kernel(cur_ref, prev_ref, ab_ref, cw_ref, alog_ref, dtb_ref, ong_ref,
                o_ref, xb_ref, s_ref):
    c = pl.program_id(1)
    nseq = cur_ref.shape[0]
    nqk = GDN_HEADS * GDN_DK
    nconv = 2 * nqk + GDN_HEADS * GDN_DV
    units = [(b, p) for b in range(nseq) for p in range(GDN_HEADS // 2)]
    nu = range(len(units))
    stack = lambda f, p: jnp.concatenate([f(2 * p), f(2 * p + 1)], axis=0)

    @pl.when(c == 0)
    def _():
        s_ref[...] = jnp.zeros_like(s_ref)

    for b in range(nseq):
        _conv_window(cur_ref[b, :, :nconv], prev_ref[b, :, :nconv], xb_ref.at[b], c == 0)
    qkv = [_silu(_causal_conv(xb_ref.at[b], cw_ref)) for b in range(nseq)]

    causal, _ = _causal_masks(CHUNK)
    row = lax.broadcasted_iota(jnp.int32, (2 * CHUNK, 2 * CHUNK), 0)
    lane = lax.broadcasted_iota(jnp.int32, (2 * CHUNK, 2 * CHUNK), 1)
    first = lane < CHUNK
    same_head = (row < CHUNK) == first
    t_row = jnp.where(row < CHUNK, row, row - CHUNK)
    t_col = jnp.where(first, lane, lane - CHUNK)
    causal2 = same_head & (t_row >= t_col)
    strict2 = same_head & (t_row > t_col)

    ab = [ab_ref[b] for b in range(nseq)]
    beta_all = [_sigmoid(t) for t in ab]
    G = [_chunk_cumsum(-jnp.exp(alog_ref[...]) * _softplus(t + dtb_ref[...]), causal) for t in ab]
    G_rows = [jnp.concatenate([t, t], axis=0).T for t in G]
    eG = [jnp.exp(t) for t in G]
    eG_rev = [jnp.exp(t[CHUNK - 1:CHUNK, :] - t) for t in G]

    def heads_of(width, offset, b, p):
        return stack(lambda h: qkv[b][:, offset + h * width:offset + (h + 1) * width], p)

    q = [heads_of(GDN_DK, 0, b, p) for b, p in units]
    k = [heads_of(GDN_DK, nqk, b, p) for b, p in units]
    v = [heads_of(GDN_DV, 2 * nqk, b, p) for b, p in units]
    squares = jnp.concatenate([(t * t).astype(BF16) for t in q + k], axis=0)
    inv_norm = lax.rsqrt(_dot(squares, jnp.ones((GDN_DK, GDN_DK), BF16)) + L2_EPS)
    rows = 2 * CHUNK
    q = [q[u] * inv_norm[u * rows:(u + 1) * rows] * (GDN_DK ** -0.5) for u in nu]
    k = [k[u] * inv_norm[(len(units) + u) * rows:(len(units) + u + 1) * rows] for u in nu]
    col_of = lambda a, b, p, off=0: jnp.broadcast_to(
        stack(lambda h: a[b][:, off + h:off + h + 1], p), (2 * CHUNK, LANES))
    beta = [col_of(beta_all, b, p, GDN_HEADS) for b, p in units]
    eg = [col_of(eG, b, p) for b, p in units]
    decay = []
    for b, p in units:
        g_row = jnp.where(first[0:1], G_rows[b][2 * p:2 * p + 1, :], G_rows[b][2 * p + 1:2 * p + 2, :])
        diff = col_of(G, b, p) - g_row
        decay.append(jnp.where(causal2, jnp.exp(jnp.where(causal2, diff, 0.0)), 0.0))
    kb = [t.astype(BF16) for t in k]
    qkk = [_dot_nt(jnp.concatenate([q[u].astype(BF16), kb[u]], axis=0), kb[u]) for u in nu]
    t_inv = _unit_lower_inverses(
        [jnp.where(strict2, beta[u] * qkk[u][rows:] * decay[u], 0.0) for u in nu], CHUNK)
    uw = [_dot(t_inv[u].astype(BF16),
               jnp.concatenate([(v[u] * beta[u]).astype(BF16),
                                (k[u] * (beta[u] * eg[u])).astype(BF16)], axis=1))
          for u in nu]
    qe = [q[u] * eg[u] for u in nu]
    halves = (slice(0, CHUNK), slice(CHUNK, rows))
    S = [[s_ref[b, 2 * p + i] for i in range(2)] for b, p in units]
    Sb = [[t.astype(BF16) for t in pair] for pair in S]
    ws = [[_dot(jnp.concatenate([uw[u][halves[i], GDN_DV:], qe[u][halves[i]]], axis=0).astype(BF16),
                Sb[u][i]) for i in range(2)] for u in nu]
    vb = [(uw[u][:, :GDN_DV] - jnp.concatenate([ws[u][0][:CHUNK], ws[u][1][:CHUNK]], axis=0))
          .astype(BF16) for u in nu]
    o = [jnp.concatenate([ws[u][0][CHUNK:], ws[u][1][CHUNK:]], axis=0)
         + _dot((qkk[u][:rows] * decay[u]).astype(BF16), vb[u]) for u in nu]
    k_dec = [(k[u] * col_of(eG_rev, b, p)).astype(BF16) for u, (b, p) in enumerate(units)]
    s_new = [[eG[b][CHUNK - 1:CHUNK, 2 * p + i:2 * p + i + 1] * S[u][i]
              + _dot_tn(k_dec[u][halves[i]], vb[u][halves[i]]) for i in range(2)]
             for u, (b, p) in enumerate(units)]
    for u, (b, p) in enumerate(units):
        for i in range(2):
            s_ref[b, 2 * p + i] = s_new[u][i]
    for u, (b, p) in enumerate(units):
        for i in range(2):
            h = 2 * p + i
            gate = cur_ref[b, :, nconv + h * GDN_DV:nconv + (h + 1) * GDN_DV].astype(F32)
            o_ref[b, :, h * GDN_DV:(h + 1) * GDN_DV] = (
                _rms(o[u][halves[i]], ong_ref[...]) * _silu(gate)).astype(o_ref.dtype)


def gdn_core(proj, ab, conv_w, a_log, dt_bias, o_norm_g):
    batch, seq_len, W = proj.shape
    nv = GDN_HEADS * GDN_DV
    nconv = conv_w.shape[1]
    nb = SEQS_PER_STEP
    pad = lambda a: jnp.pad(a.reshape(1, -1), ((0, 0), (0, LANES - a.shape[-1])))
    const = lambda shape: pl.BlockSpec(shape, lambda b, c: (0, 0))
    return pl.pallas_call(
        _gdn_kernel, grid=(batch // nb, seq_len // CHUNK),
        in_specs=[pl.BlockSpec((nb, CHUNK, W), lambda b, c: (b, c, 0)),
                  _prev_rows_spec(W),
                  pl.BlockSpec((nb, CHUNK, LANES), lambda b, c: (b, c, 0)),
                  const((CONV_W, nconv)), const((1, LANES)), const((1, LANES)),
                  const((1, GDN_DV))],
        out_specs=pl.BlockSpec((nb, CHUNK, nv), lambda b, c: (b, c, 0)),
        out_shape=jax.ShapeDtypeStruct((batch, seq_len, nv), BF16),
        scratch_shapes=[pltpu.VMEM((nb, HALO + CHUNK, nconv), BF16),
                        pltpu.VMEM((nb, GDN_HEADS, GDN_DK, GDN_DV), F32)],
        compiler_params=_cparams("parallel", "arbitrary"), name="gdn_core",
    )(proj, proj, ab, conv_w, pad(a_log), pad(dt_bias), o_norm_g.reshape(1, GDN_DV))


def gdn_mixer(h, g_mix, w_in_stack, layer, conv_w, a_log, dt_bias, o_norm_g, *, batch, seq_len):
    nmain = conv_w.shape[1] + GDN_HEADS * GDN_DV
    w_small = jnp.pad(w_in_stack[layer][:, nmain:], ((0, 0), (0, LANES - 2 * GDN_HEADS)))
    proj, ab = norm_matmul(h, g_mix, w_in_stack, layer, nmain, w_small, out_dtype=BF16)
    o = gdn_core(proj.reshape(batch, seq_len, -1), ab.reshape(batch, seq_len, -1),
                 conv_w, a_log, dt_bias, o_norm_g)
    return o.reshape(batch * seq_len, -1)


def _m2_kernel(cur_ref, prev_ref, dt_ref, cw_ref, cb_ref, dtb_ref, alog_ref, dsk_ref,
               ng_ref, sel_ref, o_ref, xb_ref, s_ref):
    c = pl.program_id(1)
    nseq = cur_ref.shape[0]
    seqs = range(nseq)
    inner = dsk_ref.shape[1]
    nbc = M2_GROUPS * M2_STATE
    pair_w = 2 * M2_HEAD_DIM
    pairs_per_group = inner // M2_GROUPS // pair_w
    gw = inner // M2_GROUPS

    @pl.when(c == 0)
    def _():
        s_ref[...] = jnp.zeros_like(s_ref)

    for b in seqs:
        _conv_window(cur_ref[b, :, inner:], prev_ref[b, :, inner:], xb_ref.at[b], c == 0)
    xbc = [_silu(_causal_conv(xb_ref.at[b], cw_ref) + cb_ref[...]) for b in seqs]
    x = [t[:, :inner] for t in xbc]

    causal, _ = _causal_masks(CHUNK)
    dt = [_softplus(dt_ref[b] + dtb_ref[...]) for b in seqs]
    cum = [_chunk_cumsum(t * (-jnp.exp(alog_ref[...])), causal) for t in dt]
    cum_rows = [jnp.concatenate([t, t], axis=0).T for t in cum]
    cum_hi, cum_lo = zip(*[_split2(t) for t in cum])
    scales = [a.astype(BF16) for b in seqs
              for a in (dt[b], jnp.exp(cum[b]), dt[b] * jnp.exp(cum[b][CHUNK - 1:CHUNK, :] - cum[b]))]
    spread = _dot(jnp.concatenate(list(cum_hi) + list(cum_lo) + scales, axis=0), sel_ref[...])
    block = lambda n: spread[n * CHUNK:(n + 1) * CHUNK]
    cum_x = [block(b) + block(nseq + b) for b in seqs]
    xdt = [x[b] * block(2 * nseq + 3 * b) for b in seqs]
    ecum = [block(2 * nseq + 3 * b + 1) for b in seqs]
    xd = [x[b] * block(2 * nseq + 3 * b + 2) for b in seqs]

    lane = lax.broadcasted_iota(jnp.int32, (CHUNK, pair_w), 1)
    row = lax.broadcasted_iota(jnp.int32, (CHUNK, pair_w), 0)
    first = lane < M2_HEAD_DIM
    causal2 = row >= jnp.where(first, lane, lane - M2_HEAD_DIM)

    units = [(b, g) for b in seqs for g in range(M2_GROUPS)]
    gcol = lambda g: slice(g * gw, (g + 1) * gw)
    Bg = [xbc[b][:, inner + g * M2_STATE:inner + (g + 1) * M2_STATE].astype(BF16) for b, g in units]
    Cg = [xbc[b][:, inner + nbc + g * M2_STATE:inner + nbc + (g + 1) * M2_STATE].astype(BF16)
          for b, g in units]
    cb2 = [_dot_nt(Cg[u], jnp.concatenate([Bg[u], Bg[u]], axis=0))
           for u in range(len(units))]
    S = [s_ref[b, g] for b, g in units]
    y_off = [_dot(Cg[u], S[u].astype(BF16)) * ecum[b][:, gcol(g)] for u, (b, g) in enumerate(units)]
    s_new = [ecum[b][CHUNK - 1:CHUNK, gcol(g)] * S[u]
             + _dot_tn(Bg[u], xd[b][:, gcol(g)].astype(BF16)) for u, (b, g) in enumerate(units)]
    for u, (b, g) in enumerate(units):
        s_ref[b, g] = s_new[u]
    y_diag = []
    for u, (b, g) in enumerate(units):
        for m in range(pairs_per_group):
            p = g * pairs_per_group + m
            cols = slice(p * pair_w, (p + 1) * pair_w)
            h1 = 2 * p
            row_pair = jnp.where(first[0:1], cum_rows[b][h1:h1 + 1, :], cum_rows[b][h1 + 1:h1 + 2, :])
            diff = cum_x[b][:, cols] - row_pair
            lmat = jnp.where(causal2, jnp.exp(jnp.where(causal2, diff, 0.0)), 0.0)
            xp = xdt[b][:, cols]
            x_bd = jnp.concatenate([jnp.where(first, xp, 0.0), jnp.where(first, 0.0, xp)], axis=0)
            y_diag.append(_dot((cb2[u] * lmat).astype(BF16), x_bd.astype(BF16)))
    for u, (b, g) in enumerate(units):
        cols = gcol(g)
        yd = jnp.concatenate(y_diag[u * pairs_per_group:(u + 1) * pairs_per_group], axis=1)
        y = ((yd + y_off[u] + dsk_ref[:, cols] * x[b][:, cols])
             * _silu(cur_ref[b, :, cols].astype(F32)))
        o_ref[b, :, cols] = _rms(y, ng_ref[:, cols]).astype(o_ref.dtype)


def m2_core(proj, dt_raw, conv_w, conv_b, dt_bias, a_log, d_skip, norm_g):
    batch, seq_len, W = proj.shape
    nconv = conv_w.shape[1]
    inner = W - nconv
    nb = SEQS_PER_STEP
    pad = lambda a: jnp.pad(a.reshape(1, -1), ((0, 0), (0, LANES - a.shape[-1])))
    head_of_lane = jnp.arange(inner, dtype=jnp.int32) // M2_HEAD_DIM
    sel = (jnp.arange(LANES, dtype=jnp.int32)[:, None] == head_of_lane[None, :]).astype(BF16)
    d_x = jnp.repeat(d_skip, M2_HEAD_DIM).reshape(1, inner)
    const = lambda shape: pl.BlockSpec(shape, lambda b, c: (0, 0))
    return pl.pallas_call(
        _m2_kernel, grid=(batch // nb, seq_len // CHUNK),
        in_specs=[pl.BlockSpec((nb, CHUNK, W), lambda b, c: (b, c, 0)),
                  _prev_rows_spec(W),
                  pl.BlockSpec((nb, CHUNK, LANES), lambda b, c: (b, c, 0)),
                  const((CONV_W, nconv)), const((1, nconv)), const((1, LANES)),
                  const((1, LANES)), const((1, inner)), const((1, inner)),
                  const((LANES, inner))],
        out_specs=pl.BlockSpec((nb, CHUNK, inner), lambda b, c: (b, c, 0)),
        out_shape=jax.ShapeDtypeStruct((batch, seq_len, inner), BF16),
        scratch_shapes=[pltpu.VMEM((nb, HALO + CHUNK, nconv), BF16),
                        pltpu.VMEM((nb, M2_GROUPS, M2_STATE, inner // M2_GROUPS), F32)],
        compiler_params=_cparams("parallel", "arbitrary"), name="m2_core",
    )(proj, proj, dt_raw, conv_w, conv_b.reshape(1, nconv), pad(dt_bias), pad(a_log),
      d_x, norm_g.reshape(1, inner), sel)


def m2_mixer(h, g_mix, w_in_stack, layer, conv_w, conv_b, dt_bias, a_log, d_skip, norm_g,
             *, batch, seq_len):
    heads = dt_bias.shape[0]
    nmain = w_in_stack.shape[2] - heads
    w_small = jnp.pad(w_in_stack[layer][:, nmain:], ((0, 0), (0, LANES - heads)))
    proj, dt_raw = norm_matmul(h, g_mix, w_in_stack, layer, nmain, w_small, out_dtype=BF16)
    y = m2_core(proj.reshape(batch, seq_len, -1), dt_raw.reshape(batch, seq_len, -1),
                conv_w, conv_b, dt_bias, a_log, d_skip, norm_g)
    return y.reshape(batch * seq_len, -1)


def _s5_prep_kernel(lr_ref, li_ref, ldt_ref, br_ref, bi_ref, lbr_ref, lbi_ref, bbr_ref, bbi_ref):
    lr, li = lr_ref[...], li_ref[...]
    dt = jnp.exp(ldt_ref[...])
    mag = jnp.exp(lr * dt)
    lbr = mag * jnp.cos(li * dt)
    lbi = mag * jnp.sin(li * dt)
    nr = lbr - 1.0
    den = lr * lr + li * li
    qr = (nr * lr + lbi * li) / den
    qi = (lbi * lr - nr * li) / den
    br, bi = br_ref[...], bi_ref[...]
    lbr_ref[...] = lbr
    lbi_ref[...] = lbi
    bbr_ref[...] = qr * br - qi * bi
    bbi_ref[...] = qr * bi + qi * br


def _s5_kernel(u_ref, b_ref, cre_ref, cim_ref, lam_ref, d_ref, y_ref, bu_ref, st_ref, *, steps):
    nslab = b_ref.shape[0]
    half = b_ref.shape[2] // 2

    @pl.when(pl.program_id(0) == 0)
    def _():
        st_ref[...] = jnp.zeros_like(st_ref)

    batch, _, d_model = u_ref.shape
    rows = batch * steps

    dst = lax.broadcasted_iota(jnp.int32, (rows, rows), 0)
    src = lax.broadcasted_iota(jnp.int32, (rows, rows), 1)
    perm = jnp.where((dst // batch == src % steps) & (dst % batch == src // steps),
                     1.0, 0.0).astype(BF16)
    u_hi, u_lo = _split2(u_ref[...].reshape(rows, d_model))
    u_hi = _dot(perm, u_hi)
    u_tm = u_hi + _dot(perm, u_lo)
    ub = u_hi.astype(BF16)

    for s in range(nslab):
        bu_ref[:, 2 * half * s:2 * half * (s + 1)] = _dot(
            ub[:, s * S5_SLAB:(s + 1) * S5_SLAB], b_ref[s])

    for s in range(nslab):
        re = slice(2 * half * s, 2 * half * s + half)
        im = slice(2 * half * s + half, 2 * half * (s + 1))
        lam_r = jnp.broadcast_to(lam_ref[:, re], (batch, half))
        lam_i = jnp.broadcast_to(lam_ref[:, im], (batch, half))

        def step(t, carry, re=re, im=im, lam_r=lam_r, lam_i=lam_i):
            xr, xi = carry
            at_t = pl.ds(pl.multiple_of(t * batch, batch), batch)
            nr = lam_r * xr - lam_i * xi + bu_ref[at_t, re]
            ni = lam_r * xi + lam_i * xr + bu_ref[at_t, im]
            bu_ref[at_t, re] = nr
            bu_ref[at_t, im] = ni
            return nr, ni

        xr, xi = lax.fori_loop(0, steps, step, (st_ref[:, re], st_ref[:, im]))
        st_ref[:, re] = xr
        st_ref[:, im] = xi

    ys = []
    for s in range(nslab):
        re = slice(2 * half * s, 2 * half * s + half)
        im = slice(2 * half * s + half, 2 * half * (s + 1))
        cols = slice(s * S5_SLAB, (s + 1) * S5_SLAB)
        y = (_dot(bu_ref[:, re].astype(BF16), cre_ref[s])
             - _dot(bu_ref[:, im].astype(BF16), cim_ref[s])
             + d_ref[:, cols] * u_tm[:, cols])
        ys.append(jax.nn.gelu(y, approximate=True).astype(BF16))
    y_sm = _dot_tn(perm, jnp.concatenate(ys, axis=1))
    y_ref[...] = y_sm.astype(y_ref.dtype).reshape(batch, steps, d_model)


def s5_core(u, lam_re, lam_im, log_dt, b_re, b_im, c_re, c_im, d_skip, *, steps=32):
    batch, L, D = u.shape
    assert batch == SUBLANES, "one time step's batch entries fill one sublane tile"
    steps = min(steps, L)
    G, P = lam_re.shape
    K = S5_GROUP
    nslab = D // S5_SLAB
    gps = S5_SLAB // K
    half = gps * P

    rep = lambda a: jnp.repeat(a, K, axis=-1)
    flat = jax.ShapeDtypeStruct((G, P * K), F32)
    lbr, lbi, bbr, bbi = pl.pallas_call(
        _s5_prep_kernel, out_shape=[flat] * 4, name="s5_prep",
    )(rep(lam_re), rep(lam_im), jnp.broadcast_to(log_dt[:, None], (G, P * K)),
      b_re.reshape(G, P * K), b_im.reshape(G, P * K))

    eye = jnp.eye(gps, dtype=F32)
    to_slab = lambda a: a.reshape(G, P, K).transpose(0, 2, 1).reshape(nslab, gps, K, P)
    b_blk = jnp.concatenate(
        [jnp.einsum('sjkp,jJ->sjkJp', to_slab(a), eye).reshape(nslab, S5_SLAB, half)
         for a in (bbr, bbi)], axis=-1).astype(BF16)
    c_blk = [jnp.einsum('sjkp,jJ->sJpjk', a.reshape(nslab, gps, K, P), eye)
             .reshape(nslab, half, S5_SLAB).astype(BF16) for a in (c_re, c_im)]
    lam = jnp.stack([lbr[:, ::K].reshape(nslab, half), lbi[:, ::K].reshape(nslab, half)],
                    axis=1).reshape(1, nslab * 2 * half)

    const3 = lambda shape: pl.BlockSpec(shape, lambda i: (0, 0, 0))
    return pl.pallas_call(
        functools.partial(_s5_kernel, steps=steps), grid=(L // steps,),
        in_specs=[pl.BlockSpec((batch, steps, D), lambda i: (0, i, 0)),
                  const3((nslab, S5_SLAB, 2 * half)),
                  const3((nslab, half, S5_SLAB)), const3((nslab, half, S5_SLAB)),
                  pl.BlockSpec((1, nslab * 2 * half), lambda i: (0, 0)),
                  pl.BlockSpec((1, D), lambda i: (0, 0))],
        out_specs=pl.BlockSpec((batch, steps, D), lambda i: (0, i, 0)),
        out_shape=jax.ShapeDtypeStruct((batch, L, D), BF16),
        scratch_shapes=[pltpu.VMEM((batch * steps, nslab * 2 * half), F32),
                        pltpu.VMEM((batch, nslab * 2 * half), F32)],
        compiler_params=_cparams("arbitrary"), name="s5_core",
    )(u, b_blk, c_blk[0], c_blk[1], lam, d_skip.reshape(1, D))


def s5_mixer(h, g_mix, w_in_stack, layer, lam_re, lam_im, log_dt, b_re, b_im, c_re, c_im, d_skip,
             *, batch, seq_len):
    T, D = h.shape
    u = norm_matmul(h, g_mix, w_in_stack, layer, D, out_dtype=F32)
    y = s5_core(u.reshape(batch, seq_len, D), lam_re, lam_im, log_dt, b_re, b_im, c_re, c_im,
                d_skip)
    return y.reshape(T, D)


def kernel(x, norm_mix_g, norm_mlp_g, mlp_w1, mlp_w2, gdn_w_in, gdn_conv_w, gdn_a_log, gdn_dt_bias, gdn_o_norm_g, gdn_w_out, s5_w_in, s5_lam_re, s5_lam_im, s5_log_dt, s5_b_re, s5_b_im, s5_c_re, s5_c_im, s5_d, s5_w_out, m2_w_in, m2_conv_w, m2_conv_b, m2_dt_bias, m2_a_log, m2_d, m2_norm_g, m2_w_out, final_norm_g):
    batch, seq_len, d_model = x.shape
    depth = norm_mix_g.shape[0]
    h = x.reshape(batch * seq_len, d_model)
    mlp_w1, mlp_w2, gdn_w_in, s5_w_in, m2_w_in = (
        w.astype(BF16) for w in (mlp_w1, mlp_w2, gdn_w_in, s5_w_in, m2_w_in))
    for i in range(depth):
        kind, j = i % 3, i // 3
        if kind == 0:
            mix = gdn_mixer(h, norm_mix_g[i], gdn_w_in, j, gdn_conv_w[j], gdn_a_log[j],
                            gdn_dt_bias[j], gdn_o_norm_g[j], batch=batch, seq_len=seq_len)
            w_out = gdn_w_out[j]
        elif kind == 1:
            mix = s5_mixer(h, norm_mix_g[i], s5_w_in, j, s5_lam_re[j], s5_lam_im[j], s5_log_dt[j],
                           s5_b_re[j], s5_b_im[j], s5_c_re[j], s5_c_im[j], s5_d[j],
                           batch=batch, seq_len=seq_len)
            w_out = s5_w_out[j]
        else:
            mix = m2_mixer(h, norm_mix_g[i], m2_w_in, j, m2_conv_w[j], m2_conv_b[j], m2_dt_bias[j],
                           m2_a_log[j], m2_d[j], m2_norm_g[j], batch=batch, seq_len=seq_len)
            w_out = m2_w_out[j]
        h = out_proj_mlp(mix, w_out.astype(BF16), h, norm_mlp_g[i], mlp_w1, mlp_w2, i,
                         final_norm_g if i == depth - 1 else None, glu=(kind == 1))
    return h.reshape(batch, seq_len, d_model)
```

```python
import functools

import jax
import jax.numpy as jnp
from jax import lax
from jax.experimental import pallas as pl
from jax.experimental.pallas import tpu as pltpu

F32 = jnp.float32
BF16 = jnp.bfloat16

RMS_EPS = 1e-6
L2_EPS = 1e-6
CHUNK = 64
CONV_W = 4
LANES = 128
SUBLANES = 8
HALO = 2 * SUBLANES
CONV_COLS = 512
SEQS_PER_STEP = 8
VMEM_LIMIT = 56 * 1024 * 1024

GDN_HEADS = 8
GDN_DK = 128
GDN_DV = 128
S5_GROUP = 16
S5_STATE = 64
S5_SLAB = 128
M2_HEAD_DIM = 64
M2_STATE = 128
M2_GROUPS = 8


def _cparams(*sem):
    return pltpu.CompilerParams(dimension_semantics=sem, vmem_limit_bytes=VMEM_LIMIT)


def _rms(x, g):
    return x * lax.rsqrt(jnp.mean(x * x, axis=-1, keepdims=True) + RMS_EPS) * g


def _sigmoid(x):
    return 0.5 * (jnp.tanh(0.5 * x) + 1.0)


def _silu(x):
    hx = 0.5 * x
    return hx + hx * jnp.tanh(hx)


def _softplus(x):
    return jnp.maximum(x, 0.0) + jnp.log(1.0 + jnp.exp(-jnp.abs(x)))


def _dot(a, b):
    return jnp.dot(a, b, preferred_element_type=F32)


def _split2(x):
    hi = x.astype(BF16)
    return hi, (x - hi.astype(F32)).astype(BF16)


def _split3(x):
    hi = x.astype(BF16)
    r = x - hi.astype(F32)
    mid = r.astype(BF16)
    return hi, mid, (r - mid.astype(F32)).astype(BF16)


def _dot_nt(a, b):
    return lax.dot_general(a, b, (((1,), (1,)), ((), ())), preferred_element_type=F32)


def _dot_tn(a, b):
    return lax.dot_general(a, b, (((0,), (0,)), ((), ())), preferred_element_type=F32)


def _causal_masks(n):
    row = lax.broadcasted_iota(jnp.int32, (n, n), 0)
    col = lax.broadcasted_iota(jnp.int32, (n, n), 1)
    return row >= col, row > col


def _norm_mm_kernel(x_ref, g_ref, w_ref, *rest, small):
    if small:
        ws_ref, o_ref, os_ref, xn_ref = rest
    else:
        o_ref, xn_ref = rest

    @pl.when(pl.program_id(1) == 0)
    def _():
        xn = _rms(x_ref[...], g_ref[...]).astype(BF16)
        xn_ref[...] = xn
        if small:
            os_ref[...] = _dot(xn, ws_ref[...])

    o_ref[...] = _dot(xn_ref[...], w_ref[...]).astype(o_ref.dtype)


def norm_matmul(h, g, w_stack, layer, n_out, w_small=None, *, batch, out_dtype, tm=1024, tn=2048):
    T, D = h.shape
    N = n_out
    seq_len = T // batch
    tm = min(tm, seq_len)
    tn = min(tn, N)
    nt = seq_len // tm
    ni, nj = T // tm, N // tn
    x_spec = pl.BlockSpec((tm, D), lambda i, j: (i, 0))
    g_spec = pl.BlockSpec((1, D), lambda i, j: (0, 0))
    w_spec = pl.BlockSpec((None, D, tn), lambda i, j: (layer, 0, j))
    out_shape = jax.ShapeDtypeStruct((batch, seq_len, N), out_dtype)
    o_spec = pl.BlockSpec((None, tm, tn), lambda i, j: (i // nt, i % nt, j))
    in_specs = [x_spec, g_spec, w_spec]
    args = [h, g.reshape(1, D), w_stack]
    small = w_small is not None
    if small:
        ns = w_small.shape[1]
        in_specs.append(pl.BlockSpec((D, ns), lambda i, j: (0, 0)))
        args.append(w_small)
        o_spec = [o_spec, pl.BlockSpec((None, tm, ns), lambda i, j: (i // nt, i % nt, 0))]
        out_shape = [out_shape, jax.ShapeDtypeStruct((batch, seq_len, ns), F32)]
    return pl.pallas_call(
        functools.partial(_norm_mm_kernel, small=small), grid=(ni, nj), in_specs=in_specs,
        out_specs=o_spec, out_shape=out_shape,
        scratch_shapes=[pltpu.VMEM((tm, D), BF16)],
        compiler_params=_cparams("parallel", "arbitrary"), name="norm_matmul",
    )(*args)


def _out_mlp_kernel(mix_ref, wo_ref, r_ref, g_ref, w1_ref, w2_ref, *rest, glu, final):
    if final:
        gf_ref, o_ref, xn_ref = rest
    else:
        o_ref, xn_ref = rest
    j = pl.program_id(1)

    @pl.when(j == 0)
    def _():
        m = _dot(mix_ref[...], wo_ref[...])
        if glu:
            n = o_ref.shape[1]
            m = m[:, :n] * _sigmoid(m[:, n:])
        h1 = r_ref[...] + m
        o_ref[...] = h1
        xn_ref[...] = _rms(h1, g_ref[...]).astype(BF16)

    a = jnp.maximum(_dot(xn_ref[...], w1_ref[...]), 0.0)
    o_ref[...] += _dot((a * a).astype(BF16), w2_ref[...])

    if final:
        @pl.when(j == pl.num_programs(1) - 1)
        def _():
            o_ref[...] = _rms(o_ref[...], gf_ref[...])


def out_proj_mlp(mix, w_out, res, g, w1_stack, w2_stack, layer, g_final=None, *,
                 glu=False, tm=1024, tf=1024):
    T, K = mix.shape
    D = res.shape[1]
    FF = w1_stack.shape[2]
    tm = min(tm, T)
    final = g_final is not None
    in_specs = [pl.BlockSpec((tm, K), lambda i, j: (i, 0)),
                pl.BlockSpec(w_out.shape, lambda i, j: (0, 0)),
                pl.BlockSpec((tm, D), lambda i, j: (i, 0)),
                pl.BlockSpec((1, D), lambda i, j: (0, 0)),
                pl.BlockSpec((None, D, tf), lambda i, j: (layer, 0, j)),
                pl.BlockSpec((None, tf, D), lambda i, j: (layer, j, 0))]
    args = [mix, w_out, res, g.reshape(1, D), w1_stack, w2_stack]
    if final:
        in_specs.append(pl.BlockSpec((1, D), lambda i, j: (0, 0)))
        args.append(g_final.reshape(1, D))
    return pl.pallas_call(
        functools.partial(_out_mlp_kernel, glu=glu, final=final), grid=(T // tm, FF // tf),
        in_specs=in_specs,
        out_specs=pl.BlockSpec((tm, D), lambda i, j: (i, 0)),
        out_shape=jax.ShapeDtypeStruct((T, D), F32),
        scratch_shapes=[pltpu.VMEM((tm, D), BF16)],
        compiler_params=_cparams("parallel", "arbitrary"), name="out_proj_mlp",
    )(*args)


def _conv_window(cur, prev, xb_ref, first_chunk):
    xb_ref[0:HALO, :] = jnp.where(first_chunk, jnp.zeros_like(prev), prev)
    xb_ref[HALO:HALO + CHUNK, :] = cur


def _causal_conv(xb_ref, cw_ref):
    window = HALO + CHUNK
    i = lax.broadcasted_iota(jnp.int32, (CHUNK, window), 0)
    c = lax.broadcasted_iota(jnp.int32, (CHUNK, window), 1)
    pick = jnp.concatenate(
        [jnp.where(c == i + (HALO - (CONV_W - 1) + w), 1.0, 0.0) for w in range(CONV_W - 1)],
        axis=1).astype(BF16)
    outs = []
    for c0 in range(0, xb_ref.shape[1], CONV_COLS):
        cols = slice(c0, c0 + CONV_COLS)
        xw = xb_ref[:, cols]
        cw = cw_ref[:, cols]
        cwb = cw.astype(BF16)
        scaled = jnp.concatenate([xw * cwb[w:w + 1] for w in range(CONV_W - 1)], axis=0)
        outs.append(_dot(pick, scaled)
                    + xw[HALO:].astype(F32) * cw[CONV_W - 1:CONV_W])
    return jnp.concatenate(outs, axis=1)


def _chunk_cumsum(x, causal):
    n = x.shape[1]
    parts = _dot(jnp.where(causal, 1.0, 0.0).astype(BF16), jnp.concatenate(_split3(x), axis=1))
    return parts[:, :n] + (parts[:, n:2 * n] + parts[:, 2 * n:])


def _prev_rows_spec(width):
    blocks_per_chunk = CHUNK // HALO
    return pl.BlockSpec(
        (SEQS_PER_STEP, HALO, width),
        lambda b, c: (b, jnp.maximum(c * blocks_per_chunk - 1, 0), 0))


def _dot_split(a, b):
    return _dot(jnp.concatenate([a[0], a[0], a[1]], axis=1),
                jnp.concatenate([b[0], b[1], b[0]], axis=0))


def _unit_lower_inverses(mats, nilpotent):
    n = mats[0].shape[0]
    row = lax.broadcasted_iota(jnp.int32, (n, n), 0)
    col = lax.broadcasted_iota(jnp.int32, (n, n), 1)
    eye = jnp.where(row == col, 1.0, 0.0)
    ps = [eye - a for a in mats]
    aks = [a.astype(BF16) for a in mats]
    power = 2
    while 2 * power < nilpotent:
        aks = [_dot(ak, ak).astype(BF16) for ak in aks]
        ps = [p + _dot(p.astype(BF16), ak) for p, ak in zip(ps, aks)]
        power *= 2
    resid = [eye - _dot_split(_split2(eye + a), _split2(p)) for a, p in zip(mats, ps)]
    return [p + _dot(p.astype(BF16), r.astype(BF16)) for p, r in zip(ps, resid)]


def _gdn_kernel(cur_ref, prev_ref, ab_ref, cw_ref, alog_ref, dtb_ref, ong_ref,
                o_ref, xb_ref, s_ref):
    c = pl.program_id(1)
    nseq = cur_ref.shape[0]
    nqk = GDN_HEADS * GDN_DK
    nconv = 2 * nqk + GDN_HEADS * GDN_DV
    units = [(b, p) for b in range(nseq) for p in range(GDN_HEADS // 2)]
    nu = range(len(units))
    stack = lambda f, p: jnp.concatenate([f(2 * p), f(2 * p + 1)], axis=0)

    @pl.when(c == 0)
    def _():
        s_ref[...] = jnp.zeros_like(s_ref)

    for b in range(nseq):
        _conv_window(cur_ref[b, :, :nconv], prev_ref[b, :, :nconv], xb_ref.at[b], c == 0)
    qkv = [_silu(_causal_conv(xb_ref.at[b], cw_ref)) for b in range(nseq)]

    causal, _ = _causal_masks(CHUNK)
    row = lax.broadcasted_iota(jnp.int32, (2 * CHUNK, 2 * CHUNK), 0)
    lane = lax.broadcasted_iota(jnp.int32, (2 * CHUNK, 2 * CHUNK), 1)
    first = lane < CHUNK
    same_head = (row < CHUNK) == first
    t_row = jnp.where(row < CHUNK, row, row - CHUNK)
    t_col = jnp.where(first, lane, lane - CHUNK)
    causal2 = same_head & (t_row >= t_col)
    strict2 = same_head & (t_row > t_col)

    ab = [ab_ref[b] for b in range(nseq)]
    beta_all = [_sigmoid(t) for t in ab]
    G = [_chunk_cumsum(-jnp.exp(alog_ref[...]) * _softplus(t + dtb_ref[...]), causal) for t in ab]
    G_rows = [jnp.concatenate([t, t], axis=0).T for t in G]
    eG = [jnp.exp(t) for t in G]
    eG_rev = [jnp.exp(t[CHUNK - 1:CHUNK, :] - t) for t in G]

    def heads_of(width, offset, b, p):
        return stack(lambda h: qkv[b][:, offset + h * width:offset + (h + 1) * width], p)

    q = [heads_of(GDN_DK, 0, b, p) for b, p in units]
    k = [heads_of(GDN_DK, nqk, b, p) for b, p in units]
    v = [heads_of(GDN_DV, 2 * nqk, b, p) for b, p in units]
    squares = jnp.concatenate([(t * t).astype(BF16) for t in q + k], axis=0)
    inv_norm = lax.rsqrt(_dot(squares, jnp.ones((GDN_DK, GDN_DK), BF16)) + L2_EPS)
    rows = 2 * CHUNK
    q = [q[u] * inv_norm[u * rows:(u + 1) * rows] * (GDN_DK ** -0.5) for u in nu]
    k = [k[u] * inv_norm[(len(units) + u) * rows:(len(units) + u + 1) * rows] for u in nu]
    col_of = lambda a, b, p, off=0: jnp.broadcast_to(
        stack(lambda h: a[b][:, off + h:off + h + 1], p), (2 * CHUNK, LANES))
    beta = [col_of(beta_all, b, p, GDN_HEADS) for b, p in units]
    eg = [col_of(eG, b, p) for b, p in units]
    decay = []
    for b, p in units:
        g_row = jnp.where(first[0:1], G_rows[b][2 * p:2 * p + 1, :], G_rows[b][2 * p + 1:2 * p + 2, :])
        diff = col_of(G, b, p) - g_row
        decay.append(jnp.where(causal2, jnp.exp(jnp.where(causal2, diff, 0.0)), 0.0))
    kb = [t.astype(BF16) for t in k]
    qkk = [_dot_nt(jnp.concatenate([q[u].astype(BF16), kb[u]], axis=0), kb[u]) for u in nu]
    t_inv = _unit_lower_inverses(
        [jnp.where(strict2, beta[u] * qkk[u][rows:] * decay[u], 0.0) for u in nu], CHUNK)
    uw = [_dot(t_inv[u].astype(BF16),
               jnp.concatenate([(v[u] * beta[u]).astype(BF16),
                                (k[u] * (beta[u] * eg[u])).astype(BF16)], axis=1))
          for u in nu]
    qe = [q[u] * eg[u] for u in nu]
    halves = (slice(0, CHUNK), slice(CHUNK, rows))
    S = [[s_ref[b, 2 * p + i] for i in range(2)] for b, p in units]
    Sb = [[t.astype(BF16) for t in pair] for pair in S]
    ws = [[_dot(jnp.concatenate([uw[u][halves[i], GDN_DV:], qe[u][halves[i]]], axis=0).astype(BF16),
                Sb[u][i]) for i in range(2)] for u in nu]
    vb = [(uw[u][:, :GDN_DV] - jnp.concatenate([ws[u][0][:CHUNK], ws[u][1][:CHUNK]], axis=0))
          .astype(BF16) for u in nu]
    o = [jnp.concatenate([ws[u][0][CHUNK:], ws[u][1][CHUNK:]], axis=0)
         + _dot((qkk[u][:rows] * decay[u]).astype(BF16), vb[u]) for u in nu]
    k_dec = [(k[u] * col_of(eG_rev, b, p)).astype(BF16) for u, (b, p) in enumerate(units)]
    s_new = [[eG[b][CHUNK - 1:CHUNK, 2 * p + i:2 * p + i + 1] * S[u][i]
              + _dot_tn(k_dec[u][halves[i]], vb[u][halves[i]]) for i in range(2)]
             for u, (b, p) in enumerate(units)]
    for u, (b, p) in enumerate(units):
        for i in range(2):
            s_ref[b, 2 * p + i] = s_new[u][i]
    for u, (b, p) in enumerate(units):
        for i in range(2):
            h = 2 * p + i
            gate = cur_ref[b, :, nconv + h * GDN_DV:nconv + (h + 1) * GDN_DV].astype(F32)
            o_ref[b, :, h * GDN_DV:(h + 1) * GDN_DV] = (
                _rms(o[u][halves[i]], ong_ref[...]) * _silu(gate)).astype(o_ref.dtype)


def gdn_core(proj, ab, conv_w, a_log, dt_bias, o_norm_g):
    batch, seq_len, W = proj.shape
    nv = GDN_HEADS * GDN_DV
    nconv = conv_w.shape[1]
    nb = SEQS_PER_STEP
    pad = lambda a: jnp.pad(a.reshape(1, -1), ((0, 0), (0, LANES - a.shape[-1])))
    const = lambda shape: pl.BlockSpec(shape, lambda b, c: (0, 0))
    return pl.pallas_call(
        _gdn_kernel, grid=(batch // nb, seq_len // CHUNK),
        in_specs=[pl.BlockSpec((nb, CHUNK, W), lambda b, c: (b, c, 0)),
                  _prev_rows_spec(W),
                  pl.BlockSpec((nb, CHUNK, LANES), lambda b, c: (b, c, 0)),
                  const((CONV_W, nconv)), const((1, LANES)), const((1, LANES)),
                  const((1, GDN_DV))],
        out_specs=pl.BlockSpec((nb, CHUNK, nv), lambda b, c: (b, c, 0)),
        out_shape=jax.ShapeDtypeStruct((batch, seq_len, nv), BF16),
        scratch_shapes=[pltpu.VMEM((nb, HALO + CHUNK, nconv), BF16),
                        pltpu.VMEM((nb, GDN_HEADS, GDN_DK, GDN_DV), F32)],
        compiler_params=_cparams("parallel", "arbitrary"), name="gdn_core",
    )(proj, proj, ab, conv_w, pad(a_log), pad(dt_bias), o_norm_g.reshape(1, GDN_DV))


def gdn_mixer(h, g_mix, w_in_stack, layer, conv_w, a_log, dt_bias, o_norm_g, *, batch, seq_len):
    nmain = conv_w.shape[1] + GDN_HEADS * GDN_DV
    w_small = jnp.pad(w_in_stack[layer][:, nmain:], ((0, 0), (0, LANES - 2 * GDN_HEADS)))
    proj, ab = norm_matmul(h, g_mix, w_in_stack, layer, nmain, w_small, batch=batch,
                           out_dtype=BF16)
    o = gdn_core(proj, ab, conv_w, a_log, dt_bias, o_norm_g)
    return o.reshape(batch * seq_len, -1)


def _m2_kernel(cur_ref, prev_ref, dt_ref, cw_ref, cb_ref, dtb_ref, alog_ref, dsk_ref,
               ng_ref, sel_ref, o_ref, xb_ref, s_ref):
    c = pl.program_id(1)
    nseq = cur_ref.shape[0]
    seqs = range(nseq)
    inner = dsk_ref.shape[1]
    nbc = M2_GROUPS * M2_STATE
    pair_w = 2 * M2_HEAD_DIM
    pairs_per_group = inner // M2_GROUPS // pair_w
    gw = inner // M2_GROUPS

    @pl.when(c == 0)
    def _():
        s_ref[...] = jnp.zeros_like(s_ref)

    for b in seqs:
        _conv_window(cur_ref[b, :, inner:], prev_ref[b, :, inner:], xb_ref.at[b], c == 0)
    xbc = [_silu(_causal_conv(xb_ref.at[b], cw_ref) + cb_ref[...]) for b in seqs]
    x = [t[:, :inner] for t in xbc]

    causal, _ = _causal_masks(CHUNK)
    dt = [_softplus(dt_ref[b] + dtb_ref[...]) for b in seqs]
    cum = [_chunk_cumsum(t * (-jnp.exp(alog_ref[...])), causal) for t in dt]
    cum_rows = [jnp.concatenate([t, t], axis=0).T for t in cum]
    cum_hi, cum_lo = zip(*[_split2(t) for t in cum])
    scales = [a.astype(BF16) for b in seqs
              for a in (dt[b], jnp.exp(cum[b]), dt[b] * jnp.exp(cum[b][CHUNK - 1:CHUNK, :] - cum[b]))]
    spread = _dot(jnp.concatenate(list(cum_hi) + list(cum_lo) + scales, axis=0), sel_ref[...])
    block = lambda n: spread[n * CHUNK:(n + 1) * CHUNK]
    cum_x = [block(b) + block(nseq + b) for b in seqs]
    xdt = [x[b] * block(2 * nseq + 3 * b) for b in seqs]
    ecum = [block(2 * nseq + 3 * b + 1) for b in seqs]
    xd = [x[b] * block(2 * nseq + 3 * b + 2) for b in seqs]

    lane = lax.broadcasted_iota(jnp.int32, (CHUNK, pair_w), 1)
    row = lax.broadcasted_iota(jnp.int32, (CHUNK, pair_w), 0)
    first = lane < M2_HEAD_DIM
    causal2 = row >= jnp.where(first, lane, lane - M2_HEAD_DIM)

    units = [(b, g) for b in seqs for g in range(M2_GROUPS)]
    gcol = lambda g: slice(g * gw, (g + 1) * gw)
    Bg = [xbc[b][:, inner + g * M2_STATE:inner + (g + 1) * M2_STATE].astype(BF16) for b, g in units]
    Cg = [xbc[b][:, inner + nbc + g * M2_STATE:inner + nbc + (g + 1) * M2_STATE].astype(BF16)
          for b, g in units]
    cb2 = [_dot_nt(Cg[u], jnp.concatenate([Bg[u], Bg[u]], axis=0))
           for u in range(len(units))]
    S = [s_ref[b, g] for b, g in units]
    y_off = [_dot(Cg[u], S[u].astype(BF16)) * ecum[b][:, gcol(g)] for u, (b, g) in enumerate(units)]
    s_new = [ecum[b][CHUNK - 1:CHUNK, gcol(g)] * S[u]
             + _dot_tn(Bg[u], xd[b][:, gcol(g)].astype(BF16)) for u, (b, g) in enumerate(units)]
    for u, (b, g) in enumerate(units):
        s_ref[b, g] = s_new[u]
    y_diag = []
    for u, (b, g) in enumerate(units):
        for m in range(pairs_per_group):
            p = g * pairs_per_group + m
            cols = slice(p * pair_w, (p + 1) * pair_w)
            h1 = 2 * p
            row_pair = jnp.where(first[0:1], cum_rows[b][h1:h1 + 1, :], cum_rows[b][h1 + 1:h1 + 2, :])
            diff = cum_x[b][:, cols] - row_pair
            lmat = jnp.where(causal2, jnp.exp(jnp.where(causal2, diff, 0.0)), 0.0)
            xp = xdt[b][:, cols]
            x_bd = jnp.concatenate([jnp.where(first, xp, 0.0), jnp.where(first, 0.0, xp)], axis=0)
            y_diag.append(_dot((cb2[u] * lmat).astype(BF16), x_bd.astype(BF16)))
    for u, (b, g) in enumerate(units):
        cols = gcol(g)
        yd = jnp.concatenate(y_diag[u * pairs_per_group:(u + 1) * pairs_per_group], axis=1)
        y = ((yd + y_off[u] + dsk_ref[:, cols] * x[b][:, cols])
             * _silu(cur_ref[b, :, cols].astype(F32)))
        o_ref[b, :, cols] = _rms(y, ng_ref[:, cols]).astype(o_ref.dtype)


def m2_core(proj, dt_raw, conv_w, conv_b, dt_bias, a_log, d_skip, norm_g):
    batch, seq_len, W = proj.shape
    nconv = conv_w.shape[1]
    inner = W - nconv
    nb = SEQS_PER_STEP
    pad = lambda a: jnp.pad(a.reshape(1, -1), ((0, 0), (0, LANES - a.shape[-1])))
    head_of_lane = jnp.arange(inner, dtype=jnp.int32) // M2_HEAD_DIM
    sel = (jnp.arange(LANES, dtype=jnp.int32)[:, None] == head_of_lane[None, :]).astype(BF16)
    d_x = jnp.repeat(d_skip, M2_HEAD_DIM).reshape(1, inner)
    const = lambda shape: pl.BlockSpec(shape, lambda b, c: (0, 0))
    return pl.pallas_call(
        _m2_kernel, grid=(batch // nb, seq_len // CHUNK),
        in_specs=[pl.BlockSpec((nb, CHUNK, W), lambda b, c: (b, c, 0)),
                  _prev_rows_spec(W),
                  pl.BlockSpec((nb, CHUNK, LANES), lambda b, c: (b, c, 0)),
                  const((CONV_W, nconv)), const((1, nconv)), const((1, LANES)),
                  const((1, LANES)), const((1, inner)), const((1, inner)),
                  const((LANES, inner))],
        out_specs=pl.BlockSpec((nb, CHUNK, inner), lambda b, c: (b, c, 0)),
        out_shape=jax.ShapeDtypeStruct((batch, seq_len, inner), BF16),
        scratch_shapes=[pltpu.VMEM((nb, HALO + CHUNK, nconv), BF16),
                        pltpu.VMEM((nb, M2_GROUPS, M2_STATE, inner // M2_GROUPS), F32)],
        compiler_params=_cparams("parallel", "arbitrary"), name="m2_core",
    )(proj, proj, dt_raw, conv_w, conv_b.reshape(1, nconv), pad(dt_bias), pad(a_log),
      d_x, norm_g.reshape(1, inner), sel)


def m2_mixer(h, g_mix, w_in_stack, layer, conv_w, conv_b, dt_bias, a_log, d_skip, norm_g,
             *, batch, seq_len):
    heads = dt_bias.shape[0]
    nmain = w_in_stack.shape[2] - heads
    w_small = jnp.pad(w_in_stack[layer][:, nmain:], ((0, 0), (0, LANES - heads)))
    proj, dt_raw = norm_matmul(h, g_mix, w_in_stack, layer, nmain, w_small, batch=batch,
                               out_dtype=BF16)
    y = m2_core(proj, dt_raw, conv_w, conv_b, dt_bias, a_log, d_skip, norm_g)
    return y.reshape(batch * seq_len, -1)


def _s5_prep_kernel(lr_ref, li_ref, ldt_ref, br_ref, bi_ref, lbr_ref, lbi_ref, bbr_ref, bbi_ref):
    lr, li = lr_ref[...], li_ref[...]
    dt = jnp.exp(ldt_ref[...])
    mag = jnp.exp(lr * dt)
    lbr = mag * jnp.cos(li * dt)
    lbi = mag * jnp.sin(li * dt)
    nr = lbr - 1.0
    den = lr * lr + li * li
    qr = (nr * lr + lbi * li) / den
    qi = (lbi * lr - nr * li) / den
    br, bi = br_ref[...], bi_ref[...]
    lbr_ref[...] = lbr
    lbi_ref[...] = lbi
    bbr_ref[...] = qr * br - qi * bi
    bbi_ref[...] = qr * bi + qi * br


def _s5_kernel(u_ref, b_ref, cre_ref, cim_ref, lam_ref, d_ref, y_ref, bu_ref, st_ref, *, steps):
    nslab = b_ref.shape[0]
    half = b_ref.shape[2] // 2

    @pl.when(pl.program_id(0) == 0)
    def _():
        st_ref[...] = jnp.zeros_like(st_ref)

    batch, _, d_model = u_ref.shape
    rows = batch * steps

    dst = lax.broadcasted_iota(jnp.int32, (rows, rows), 0)
    src = lax.broadcasted_iota(jnp.int32, (rows, rows), 1)
    perm = jnp.where((dst // batch == src % steps) & (dst % batch == src // steps),
                     1.0, 0.0).astype(BF16)
    u_hi, u_lo = _split2(u_ref[...].reshape(rows, d_model))
    u_hi = _dot(perm, u_hi)
    u_tm = u_hi + _dot(perm, u_lo)
    ub = u_hi.astype(BF16)

    for s in range(nslab):
        bu_ref[:, 2 * half * s:2 * half * (s + 1)] = _dot(
            ub[:, s * S5_SLAB:(s + 1) * S5_SLAB], b_ref[s])

    for s in range(nslab):
        re = slice(2 * half * s, 2 * half * s + half)
        im = slice(2 * half * s + half, 2 * half * (s + 1))
        lam_r = jnp.broadcast_to(lam_ref[:, re], (batch, half))
        lam_i = jnp.broadcast_to(lam_ref[:, im], (batch, half))

        def step(t, carry, re=re, im=im, lam_r=lam_r, lam_i=lam_i):
            xr, xi = carry
            at_t = pl.ds(pl.multiple_of(t * batch, batch), batch)
            nr = lam_r * xr - lam_i * xi + bu_ref[at_t, re]
            ni = lam_r * xi + lam_i * xr + bu_ref[at_t, im]
            bu_ref[at_t, re] = nr
            bu_ref[at_t, im] = ni
            return nr, ni

        xr, xi = lax.fori_loop(0, steps, step, (st_ref[:, re], st_ref[:, im]))
        st_ref[:, re] = xr
        st_ref[:, im] = xi

    ys = []
    for s in range(nslab):
        re = slice(2 * half * s, 2 * half * s + half)
        im = slice(2 * half * s + half, 2 * half * (s + 1))
        cols = slice(s * S5_SLAB, (s + 1) * S5_SLAB)
        y = (_dot(bu_ref[:, re].astype(BF16), cre_ref[s])
             - _dot(bu_ref[:, im].astype(BF16), cim_ref[s])
             + d_ref[:, cols] * u_tm[:, cols])
        ys.append(jax.nn.gelu(y, approximate=True).astype(BF16))
    y_sm = _dot_tn(perm, jnp.concatenate(ys, axis=1))
    y_ref[...] = y_sm.astype(y_ref.dtype).reshape(batch, steps, d_model)


def s5_core(u, lam_re, lam_im, log_dt, b_re, b_im, c_re, c_im, d_skip, *, steps=32):
    batch, L, D = u.shape
    assert batch == SUBLANES, "one time step's batch entries fill one sublane tile"
    steps = min(steps, L)
    G, P = lam_re.shape
    K = S5_GROUP
    nslab = D // S5_SLAB
    gps = S5_SLAB // K
    half = gps * P

    rep = lambda a: jnp.repeat(a, K, axis=-1)
    flat = jax.ShapeDtypeStruct((G, P * K), F32)
    lbr, lbi, bbr, bbi = pl.pallas_call(
        _s5_prep_kernel, out_shape=[flat] * 4, name="s5_prep",
    )(rep(lam_re), rep(lam_im), jnp.broadcast_to(log_dt[:, None], (G, P * K)),
      b_re.reshape(G, P * K), b_im.reshape(G, P * K))

    eye = jnp.eye(gps, dtype=F32)
    to_slab = lambda a: a.reshape(G, P, K).transpose(0, 2, 1).reshape(nslab, gps, K, P)
    b_blk = jnp.concatenate(
        [jnp.einsum('sjkp,jJ->sjkJp', to_slab(a), eye).reshape(nslab, S5_SLAB, half)
         for a in (bbr, bbi)], axis=-1).astype(BF16)
    c_blk = [jnp.einsum('sjkp,jJ->sJpjk', a.reshape(nslab, gps, K, P), eye)
             .reshape(nslab, half, S5_SLAB).astype(BF16) for a in (c_re, c_im)]
    lam = jnp.stack([lbr[:, ::K].reshape(nslab, half), lbi[:, ::K].reshape(nslab, half)],
                    axis=1).reshape(1, nslab * 2 * half)

    const3 = lambda shape: pl.BlockSpec(shape, lambda i: (0, 0, 0))
    return pl.pallas_call(
        functools.partial(_s5_kernel, steps=steps), grid=(L // steps,),
        in_specs=[pl.BlockSpec((batch, steps, D), lambda i: (0, i, 0)),
                  const3((nslab, S5_SLAB, 2 * half)),
                  const3((nslab, half, S5_SLAB)), const3((nslab, half, S5_SLAB)),
                  pl.BlockSpec((1, nslab * 2 * half), lambda i: (0, 0)),
                  pl.BlockSpec((1, D), lambda i: (0, 0))],
        out_specs=pl.BlockSpec((batch, steps, D), lambda i: (0, i, 0)),
        out_shape=jax.ShapeDtypeStruct((batch, L, D), BF16),
        scratch_shapes=[pltpu.VMEM((batch * steps, nslab * 2 * half), F32),
                        pltpu.VMEM((batch, nslab * 2 * half), F32)],
        compiler_params=_cparams("arbitrary"), name="s5_core",
    )(u, b_blk, c_blk[0], c_blk[1], lam, d_skip.reshape(1, D))


def s5_mixer(h, g_mix, w_in_stack, layer, lam_re, lam_im, log_dt, b_re, b_im, c_re, c_im, d_skip,
             *, batch, seq_len):
    T, D = h.shape
    u = norm_matmul(h, g_mix, w_in_stack, layer, D, batch=batch, out_dtype=F32)
    y = s5_core(u, lam_re, lam_im, log_dt, b_re, b_im, c_re, c_im, d_skip)
    return y.reshape(T, D)


def kernel(x, norm_mix_g, norm_mlp_g, mlp_w1, mlp_w2, gdn_w_in, gdn_conv_w, gdn_a_log, gdn_dt_bias, gdn_o_norm_g, gdn_w_out, s5_w_in, s5_lam_re, s5_lam_im, s5_log_dt, s5_b_re, s5_b_im, s5_c_re, s5_c_im, s5_d, s5_w_out, m2_w_in, m2_conv_w, m2_conv_b, m2_dt_bias, m2_a_log, m2_d, m2_norm_g, m2_w_out, final_norm_g):
    batch, seq_len, d_model = x.shape
    depth = norm_mix_g.shape[0]
    h = x.reshape(batch * seq_len, d_model)
    mlp_w1, mlp_w2, gdn_w_in, s5_w_in, m2_w_in = (
        w.astype(BF16) for w in (mlp_w1, mlp_w2, gdn_w_in, s5_w_in, m2_w_in))
    for i in range(depth):
        kind, j = i % 3, i // 3
        if kind == 0:
            mix = gdn_mixer(h, norm_mix_g[i], gdn_w_in, j, gdn_conv_w[j], gdn_a_log[j],
                            gdn_dt_bias[j], gdn_o_norm_g[j], batch=batch, seq_len=seq_len)
            w_out = gdn_w_out[j]
        elif kind == 1:
            mix = s5_mixer(h, norm_mix_g[i], s5_w_in, j, s5_lam_re[j], s5_lam_im[j], s5_log_dt[j],
                           s5_b_re[j], s5_b_im[j], s5_c_re[j], s5_c_im[j], s5_d[j],
                           batch=batch, seq_len=seq_len)
            w_out = s5_w_out[j]
        else:
            mix = m2_mixer(h, norm_mix_g[i], m2_w_in, j, m2_conv_w[j], m2_conv_b[j], m2_dt_bias[j],
                           m2_a_log[j], m2_d[j], m2_norm_g[j], batch=batch, seq_len=seq_len)
            w_out = m2_w_out[j]
        h = out_proj_mlp(mix, w_out.astype(BF16), h, norm_mlp_g[i], mlp_w1, mlp_w2, i,
                         final_norm_g if i == depth - 1 else None, glu=(kind == 1))
    return h.reshape(batch, seq_len, d_model)
```

```python
import functools

import jax
import jax.numpy as jnp
from jax import lax
from jax.experimental import pallas as pl
from jax.experimental.pallas import tpu as pltpu

F32 = jnp.float32
BF16 = jnp.bfloat16

RMS_EPS = 1e-6
L2_EPS = 1e-6
CHUNK = 64
CONV_W = 4
LANES = 128
SUBLANES = 8
HALO = 2 * SUBLANES
CONV_COLS = 512
SEQS_PER_STEP = 8
VMEM_LIMIT = 56 * 1024 * 1024

GDN_HEADS = 8
GDN_DK = 128
GDN_DV = 128
S5_GROUP = 16
S5_STATE = 64
S5_SLAB = 128
M2_HEAD_DIM = 64
M2_STATE = 128
M2_GROUPS = 8


def _cparams(*sem):
    return pltpu.CompilerParams(dimension_semantics=sem, vmem_limit_bytes=VMEM_LIMIT)


def _rms(x, g):
    return x * lax.rsqrt(jnp.mean(x * x, axis=-1, keepdims=True) + RMS_EPS) * g


def _sigmoid(x):
    return 0.5 * (jnp.tanh(0.5 * x) + 1.0)


def _silu(x):
    hx = 0.5 * x
    return hx + hx * jnp.tanh(hx)


def _softplus(x):
    return jnp.maximum(x, 0.0) + jnp.log(1.0 + jnp.exp(-jnp.abs(x)))


def _dot(a, b):
    return jnp.dot(a, b, preferred_element_type=F32)


def _split2(x):
    hi = x.astype(BF16)
    return hi, (x - hi.astype(F32)).astype(BF16)


def _split3(x):
    hi = x.astype(BF16)
    r = x - hi.astype(F32)
    mid = r.astype(BF16)
    return hi, mid, (r - mid.astype(F32)).astype(BF16)


def _dot_nt(a, b):
    return lax.dot_general(a, b, (((1,), (1,)), ((), ())), preferred_element_type=F32)


def _dot_tn(a, b):
    return lax.dot_general(a, b, (((0,), (0,)), ((), ())), preferred_element_type=F32)


def _causal_masks(n):
    row = lax.broadcasted_iota(jnp.int32, (n, n), 0)
    col = lax.broadcasted_iota(jnp.int32, (n, n), 1)
    return row >= col, row > col


def _norm_mm_kernel(x_ref, w_ref, *rest, small):
    if small:
        ws_ref, o_ref, os_ref, xb_ref, r_ref = rest
    else:
        o_ref, xb_ref, r_ref = rest
    j = pl.program_id(1)

    @pl.when(j == 0)
    def _():
        x = x_ref[...]
        xb = x.astype(BF16)
        xb_ref[...] = xb
        r = lax.rsqrt(jnp.mean(x * x, axis=-1, keepdims=True) + RMS_EPS)
        r_ref[...] = jnp.broadcast_to(r, r_ref.shape)
        if small:
            os_ref[...] = _dot(xb, ws_ref[...]) * r
        o_ref[...] = (_dot(xb, w_ref[...]) * r).astype(o_ref.dtype)

    @pl.when(j > 0)
    def _():
        o_ref[...] = (_dot(xb_ref[...], w_ref[...]) * r_ref[:, 0:1]).astype(o_ref.dtype)


def norm_matmul(h, w_stack, layer, n_out, w_small=None, *, batch, out_dtype, tm=1024, tn=2048):
    T, D = h.shape
    N = n_out
    seq_len = T // batch
    tm = min(tm, seq_len)
    tn = min(tn, N)
    nt = seq_len // tm
    ni, nj = T // tm, N // tn
    x_spec = pl.BlockSpec((tm, D), lambda i, j: (i, 0))
    w_spec = pl.BlockSpec((None, D, tn), lambda i, j: (layer, 0, j))
    out_shape = jax.ShapeDtypeStruct((batch, seq_len, N), out_dtype)
    o_spec = pl.BlockSpec((None, tm, tn), lambda i, j: (i // nt, i % nt, j))
    in_specs = [x_spec, w_spec]
    args = [h, w_stack]
    small = w_small is not None
    if small:
        ns = w_small.shape[1]
        in_specs.append(pl.BlockSpec((D, ns), lambda i, j: (0, 0)))
        args.append(w_small)
        o_spec = [o_spec, pl.BlockSpec((None, tm, ns), lambda i, j: (i // nt, i % nt, 0))]
        out_shape = [out_shape, jax.ShapeDtypeStruct((batch, seq_len, ns), F32)]
    return pl.pallas_call(
        functools.partial(_norm_mm_kernel, small=small), grid=(ni, nj), in_specs=in_specs,
        out_specs=o_spec, out_shape=out_shape,
        scratch_shapes=[pltpu.VMEM((tm, D), BF16), pltpu.VMEM((tm, LANES), F32)],
        compiler_params=_cparams("parallel", "arbitrary"), name="norm_matmul",
    )(*args)


def _out_mlp_kernel(mix_ref, wo_ref, r_ref, g_ref, w1_ref, w2_ref, *rest, glu, final):
    if final:
        gf_ref, o_ref, xn_ref = rest
    else:
        o_ref, xn_ref = rest
    j = pl.program_id(1)

    @pl.when(j == 0)
    def _():
        m = _dot(mix_ref[...], wo_ref[...])
        if glu:
            n = o_ref.shape[1]
            m = m[:, :n] * _sigmoid(m[:, n:])
        h1 = r_ref[...] + m
        o_ref[...] = h1
        xn_ref[...] = _rms(h1, g_ref[...]).astype(BF16)

    a = jnp.maximum(_dot(xn_ref[...], w1_ref[...]), 0.0)
    o_ref[...] += _dot((a * a).astype(BF16), w2_ref[...])

    if final:
        @pl.when(j == pl.num_programs(1) - 1)
        def _():
            o_ref[...] = _rms(o_ref[...], gf_ref[...])


def out_proj_mlp(mix, w_out, res, g, w1_stack, w2_stack, layer, g_final=None, *,
                 glu=False, tm=1024, tf=1024):
    T, K = mix.shape
    D = res.shape[1]
    FF = w1_stack.shape[2]
    tm = min(tm, T)
    final = g_final is not None
    in_specs = [pl.BlockSpec((tm, K), lambda i, j: (i, 0)),
                pl.BlockSpec(w_out.shape, lambda i, j: (0, 0)),
                pl.BlockSpec((tm, D), lambda i, j: (i, 0)),
                pl.BlockSpec((1, D), lambda i, j: (0, 0)),
                pl.BlockSpec((None, D, tf), lambda i, j: (layer, 0, j)),
                pl.BlockSpec((None, tf, D), lambda i, j: (layer, j, 0))]
    args = [mix, w_out, res, g.reshape(1, D), w1_stack, w2_stack]
    if final:
        in_specs.append(pl.BlockSpec((1, D), lambda i, j: (0, 0)))
        args.append(g_final.reshape(1, D))
    return pl.pallas_call(
        functools.partial(_out_mlp_kernel, glu=glu, final=final), grid=(T // tm, FF // tf),
        in_specs=in_specs,
        out_specs=pl.BlockSpec((tm, D), lambda i, j: (i, 0)),
        out_shape=jax.ShapeDtypeStruct((T, D), F32),
        scratch_shapes=[pltpu.VMEM((tm, D), BF16)],
        compiler_params=_cparams("parallel", "arbitrary"), name="out_proj_mlp",
    )(*args)


def _conv_window(cur, prev, xb_ref, first_chunk):
    xb_ref[0:HALO, :] = jnp.where(first_chunk, jnp.zeros_like(prev), prev)
    xb_ref[HALO:HALO + CHUNK, :] = cur


def _causal_conv(xb_ref, cw_ref):
    window = HALO + CHUNK
    i = lax.broadcasted_iota(jnp.int32, (CHUNK, window), 0)
    c = lax.broadcasted_iota(jnp.int32, (CHUNK, window), 1)
    pick = jnp.concatenate(
        [jnp.where(c == i + (HALO - (CONV_W - 1) + w), 1.0, 0.0) for w in range(CONV_W - 1)],
        axis=1).astype(BF16)
    outs = []
    for c0 in range(0, xb_ref.shape[1], CONV_COLS):
        cols = slice(c0, c0 + CONV_COLS)
        xw = xb_ref[:, cols]
        cw = cw_ref[:, cols]
        cwb = cw.astype(BF16)
        scaled = jnp.concatenate([xw * cwb[w:w + 1] for w in range(CONV_W - 1)], axis=0)
        outs.append(_dot(pick, scaled)
                    + xw[HALO:].astype(F32) * cw[CONV_W - 1:CONV_W])
    return jnp.concatenate(outs, axis=1)


def _chunk_cumsum(x, causal):
    n = x.shape[1]
    parts = _dot(jnp.where(causal, 1.0, 0.0).astype(BF16), jnp.concatenate(_split3(x), axis=1))
    return parts[:, :n] + (parts[:, n:2 * n] + parts[:, 2 * n:])


def _prev_rows_spec(width):
    blocks_per_chunk = CHUNK // HALO
    return pl.BlockSpec(
        (SEQS_PER_STEP, HALO, width),
        lambda b, c: (b, jnp.maximum(c * blocks_per_chunk - 1, 0), 0))


def _dot_split(a, b):
    return _dot(jnp.concatenate([a[0], a[0], a[1]], axis=1),
                jnp.concatenate([b[0], b[1], b[0]], axis=0))


def _unit_lower_inverses(mats, nilpotent):
    n = mats[0].shape[0]
    row = lax.broadcasted_iota(jnp.int32, (n, n), 0)
    col = lax.broadcasted_iota(jnp.int32, (n, n), 1)
    eye = jnp.where(row == col, 1.0, 0.0)
    ps = [eye - a for a in mats]
    aks = [a.astype(BF16) for a in mats]
    power = 2
    while 2 * power < nilpotent:
        aks = [_dot(ak, ak).astype(BF16) for ak in aks]
        ps = [p + _dot(p.astype(BF16), ak) for p, ak in zip(ps, aks)]
        power *= 2
    resid = [eye - _dot_split(_split2(eye + a), _split2(p)) for a, p in zip(mats, ps)]
    return [p + _dot(p.astype(BF16), r.astype(BF16)) for p, r in zip(ps, resid)]


def _gdn_kernel(cur_ref, prev_ref, ab_ref, cw_ref, alog_ref, dtb_ref, ong_ref,
                o_ref, xb_ref, s_ref):
    c = pl.program_id(1)
    nseq = cur_ref.shape[0]
    nqk = GDN_HEADS * GDN_DK
    nconv = 2 * nqk + GDN_HEADS * GDN_DV
    units = [(b, p) for b in range(nseq) for p in range(GDN_HEADS // 2)]
    nu = range(len(units))
    stack = lambda f, p: jnp.concatenate([f(2 * p), f(2 * p + 1)], axis=0)

    @pl.when(c == 0)
    def _():
        s_ref[...] = jnp.zeros_like(s_ref)

    for b in range(nseq):
        _conv_window(cur_ref[b, :, :nconv], prev_ref[b, :, :nconv], xb_ref.at[b], c == 0)
    qkv = [_silu(_causal_conv(xb_ref.at[b], cw_ref)) for b in range(nseq)]

    causal, _ = _causal_masks(CHUNK)
    row = lax.broadcasted_iota(jnp.int32, (2 * CHUNK, 2 * CHUNK), 0)
    lane = lax.broadcasted_iota(jnp.int32, (2 * CHUNK, 2 * CHUNK), 1)
    first = lane < CHUNK
    same_head = (row < CHUNK) == first
    t_row = jnp.where(row < CHUNK, row, row - CHUNK)
    t_col = jnp.where(first, lane, lane - CHUNK)
    causal2 = same_head & (t_row >= t_col)
    strict2 = same_head & (t_row > t_col)

    ab = [ab_ref[b] for b in range(nseq)]
    beta_all = [_sigmoid(t) for t in ab]
    G = [_chunk_cumsum(-jnp.exp(alog_ref[...]) * _softplus(t + dtb_ref[...]), causal) for t in ab]
    G_rows = [jnp.concatenate([t, t], axis=0).T for t in G]
    eG = [jnp.exp(t) for t in G]
    eG_rev = [jnp.exp(t[CHUNK - 1:CHUNK, :] - t) for t in G]

    def heads_of(width, offset, b, p):
        return stack(lambda h: qkv[b][:, offset + h * width:offset + (h + 1) * width], p)

    q = [heads_of(GDN_DK, 0, b, p) for b, p in units]
    k = [heads_of(GDN_DK, nqk, b, p) for b, p in units]
    v = [heads_of(GDN_DV, 2 * nqk, b, p) for b, p in units]
    squares = jnp.concatenate([(t * t).astype(BF16) for t in q + k], axis=0)
    inv_norm = lax.rsqrt(_dot(squares, jnp.ones((GDN_DK, GDN_DK), BF16)) + L2_EPS)
    rows = 2 * CHUNK
    q = [q[u] * inv_norm[u * rows:(u + 1) * rows] * (GDN_DK ** -0.5) for u in nu]
    k = [k[u] * inv_norm[(len(units) + u) * rows:(len(units) + u + 1) * rows] for u in nu]
    col_of = lambda a, b, p, off=0: jnp.broadcast_to(
        stack(lambda h: a[b][:, off + h:off + h + 1], p), (2 * CHUNK, LANES))
    beta = [col_of(beta_all, b, p, GDN_HEADS) for b, p in units]
    eg = [col_of(eG, b, p) for b, p in units]
    decay = []
    for b, p in units:
        g_row = jnp.where(first[0:1], G_rows[b][2 * p:2 * p + 1, :], G_rows[b][2 * p + 1:2 * p + 2, :])
        diff = col_of(G, b, p) - g_row
        decay.append(jnp.where(causal2, jnp.exp(jnp.where(causal2, diff, 0.0)), 0.0))
    kb = [t.astype(BF16) for t in k]
    qkk = [_dot_nt(jnp.concatenate([q[u].astype(BF16), kb[u]], axis=0), kb[u]) for u in nu]
    t_inv = _unit_lower_inverses(
        [jnp.where(strict2, beta[u] * qkk[u][rows:] * decay[u], 0.0) for u in nu], CHUNK)
    uw = [_dot(t_inv[u].astype(BF16),
               jnp.concatenate([(v[u] * beta[u]).astype(BF16),
                                (k[u] * (beta[u] * eg[u])).astype(BF16)], axis=1))
          for u in nu]
    qe = [q[u] * eg[u] for u in nu]
    halves = (slice(0, CHUNK), slice(CHUNK, rows))
    S = [[s_ref[b, 2 * p + i] for i in range(2)] for b, p in units]
    Sb = [[t.astype(BF16) for t in pair] for pair in S]
    ws = [[_dot(jnp.concatenate([uw[u][halves[i], GDN_DV:], qe[u][halves[i]]], axis=0).astype(BF16),
                Sb[u][i]) for i in range(2)] for u in nu]
    vb = [(uw[u][:, :GDN_DV] - jnp.concatenate([ws[u][0][:CHUNK], ws[u][1][:CHUNK]], axis=0))
          .astype(BF16) for u in nu]
    o = [jnp.concatenate([ws[u][0][CHUNK:], ws[u][1][CHUNK:]], axis=0)
         + _dot((qkk[u][:rows] * decay[u]).astype(BF16), vb[u]) for u in nu]
    k_dec = [(k[u] * col_of(eG_rev, b, p)).astype(BF16) for u, (b, p) in enumerate(units)]
    s_new = [[eG[b][CHUNK - 1:CHUNK, 2 * p + i:2 * p + i + 1] * S[u][i]
              + _dot_tn(k_dec[u][halves[i]], vb[u][halves[i]]) for i in range(2)]
             for u, (b, p) in enumerate(units)]
    for u, (b, p) in enumerate(units):
        for i in range(2):
            s_ref[b, 2 * p + i] = s_new[u][i]
    for u, (b, p) in enumerate(units):
        for i in range(2):
            h = 2 * p + i
            gate = cur_ref[b, :, nconv + h * GDN_DV:nconv + (h + 1) * GDN_DV].astype(F32)
            o_ref[b, :, h * GDN_DV:(h + 1) * GDN_DV] = (
                _rms(o[u][halves[i]], ong_ref[...]) * _silu(gate)).astype(o_ref.dtype)


def gdn_core(proj, ab, conv_w, a_log, dt_bias, o_norm_g):
    batch, seq_len, W = proj.shape
    nv = GDN_HEADS * GDN_DV
    nconv = conv_w.shape[1]
    nb = SEQS_PER_STEP
    pad = lambda a: jnp.pad(a.reshape(1, -1), ((0, 0), (0, LANES - a.shape[-1])))
    const = lambda shape: pl.BlockSpec(shape, lambda b, c: (0, 0))
    return pl.pallas_call(
        _gdn_kernel, grid=(batch // nb, seq_len // CHUNK),
        in_specs=[pl.BlockSpec((nb, CHUNK, W), lambda b, c: (b, c, 0)),
                  _prev_rows_spec(W),
                  pl.BlockSpec((nb, CHUNK, LANES), lambda b, c: (b, c, 0)),
                  const((CONV_W, nconv)), const((1, LANES)), const((1, LANES)),
                  const((1, GDN_DV))],
        out_specs=pl.BlockSpec((nb, CHUNK, nv), lambda b, c: (b, c, 0)),
        out_shape=jax.ShapeDtypeStruct((batch, seq_len, nv), BF16),
        scratch_shapes=[pltpu.VMEM((nb, HALO + CHUNK, nconv), BF16),
                        pltpu.VMEM((nb, GDN_HEADS, GDN_DK, GDN_DV), F32)],
        compiler_params=_cparams("parallel", "arbitrary"), name="gdn_core",
    )(proj, proj, ab, conv_w, pad(a_log), pad(dt_bias), o_norm_g.reshape(1, GDN_DV))


def gdn_mixer(h, w_in_stack, layer,conv_w, a_log, dt_bias, o_norm_g, *, batch, seq_len):
    nmain = conv_w.shape[1] + GDN_HEADS * GDN_DV
    w_small = jnp.pad(w_in_stack[layer][:, nmain:], ((0, 0), (0, LANES - 2 * GDN_HEADS)))
    proj, ab = norm_matmul(h, w_in_stack, layer,nmain, w_small, batch=batch,
                           out_dtype=BF16)
    o = gdn_core(proj, ab, conv_w, a_log, dt_bias, o_norm_g)
    return o.reshape(batch * seq_len, -1)


def _m2_kernel(cur_ref, prev_ref, dt_ref, cw_ref, cb_ref, dtb_ref, alog_ref, dsk_ref,
               ng_ref, sel_ref, o_ref, xb_ref, s_ref):
    c = pl.program_id(1)
    nseq = cur_ref.shape[0]
    seqs = range(nseq)
    inner = dsk_ref.shape[1]
    nbc = M2_GROUPS * M2_STATE
    pair_w = 2 * M2_HEAD_DIM
    pairs_per_group = inner // M2_GROUPS // pair_w
    gw = inner // M2_GROUPS

    @pl.when(c == 0)
    def _():
        s_ref[...] = jnp.zeros_like(s_ref)

    for b in seqs:
        _conv_window(cur_ref[b, :, inner:], prev_ref[b, :, inner:], xb_ref.at[b], c == 0)
    xbc = [_silu(_causal_conv(xb_ref.at[b], cw_ref) + cb_ref[...]) for b in seqs]
    x = [t[:, :inner] for t in xbc]

    causal, _ = _causal_masks(CHUNK)
    dt = [_softplus(dt_ref[b] + dtb_ref[...]) for b in seqs]
    cum = [_chunk_cumsum(t * (-jnp.exp(alog_ref[...])), causal) for t in dt]
    cum_rows = [jnp.concatenate([t, t], axis=0).T for t in cum]
    cum_hi, cum_lo = zip(*[_split2(t) for t in cum])
    scales = [a.astype(BF16) for b in seqs
              for a in (dt[b], jnp.exp(cum[b]), dt[b] * jnp.exp(cum[b][CHUNK - 1:CHUNK, :] - cum[b]))]
    spread = _dot(jnp.concatenate(list(cum_hi) + list(cum_lo) + scales, axis=0), sel_ref[...])
    block = lambda n: spread[n * CHUNK:(n + 1) * CHUNK]
    cum_x = [block(b) + block(nseq + b) for b in seqs]
    xdt = [x[b] * block(2 * nseq + 3 * b) for b in seqs]
    ecum = [block(2 * nseq + 3 * b + 1) for b in seqs]
    xd = [x[b] * block(2 * nseq + 3 * b + 2) for b in seqs]

    lane = lax.broadcasted_iota(jnp.int32, (CHUNK, pair_w), 1)
    row = lax.broadcasted_iota(jnp.int32, (CHUNK, pair_w), 0)
    first = lane < M2_HEAD_DIM
    causal2 = row >= jnp.where(first, lane, lane - M2_HEAD_DIM)

    units = [(b, g) for b in seqs for g in range(M2_GROUPS)]
    gcol = lambda g: slice(g * gw, (g + 1) * gw)
    Bg = [xbc[b][:, inner + g * M2_STATE:inner + (g + 1) * M2_STATE].astype(BF16) for b, g in units]
    Cg = [xbc[b][:, inner + nbc + g * M2_STATE:inner + nbc + (g + 1) * M2_STATE].astype(BF16)
          for b, g in units]
    cb2 = [_dot_nt(Cg[u], jnp.concatenate([Bg[u], Bg[u]], axis=0))
           for u in range(len(units))]
    S = [s_ref[b, g] for b, g in units]
    y_off = [_dot(Cg[u], S[u].astype(BF16)) * ecum[b][:, gcol(g)] for u, (b, g) in enumerate(units)]
    s_new = [ecum[b][CHUNK - 1:CHUNK, gcol(g)] * S[u]
             + _dot_tn(Bg[u], xd[b][:, gcol(g)].astype(BF16)) for u, (b, g) in enumerate(units)]
    for u, (b, g) in enumerate(units):
        s_ref[b, g] = s_new[u]
    y_diag = []
    for u, (b, g) in enumerate(units):
        for m in range(pairs_per_group):
            p = g * pairs_per_group + m
            cols = slice(p * pair_w, (p + 1) * pair_w)
            h1 = 2 * p
            row_pair = jnp.where(first[0:1], cum_rows[b][h1:h1 + 1, :], cum_rows[b][h1 + 1:h1 + 2, :])
            diff = cum_x[b][:, cols] - row_pair
            lmat = jnp.where(causal2, jnp.exp(jnp.where(causal2, diff, 0.0)), 0.0)
            xp = xdt[b][:, cols]
            x_bd = jnp.concatenate([jnp.where(first, xp, 0.0), jnp.where(first, 0.0, xp)], axis=0)
            y_diag.append(_dot((cb2[u] * lmat).astype(BF16), x_bd.astype(BF16)))
    for u, (b, g) in enumerate(units):
        cols = gcol(g)
        yd = jnp.concatenate(y_diag[u * pairs_per_group:(u + 1) * pairs_per_group], axis=1)
        y = ((yd + y_off[u] + dsk_ref[:, cols] * x[b][:, cols])
             * _silu(cur_ref[b, :, cols].astype(F32)))
        o_ref[b, :, cols] = _rms(y, ng_ref[:, cols]).astype(o_ref.dtype)


def m2_core(proj, dt_raw, conv_w, conv_b, dt_bias, a_log, d_skip, norm_g):
    batch, seq_len, W = proj.shape
    nconv = conv_w.shape[1]
    inner = W - nconv
    nb = SEQS_PER_STEP
    pad = lambda a: jnp.pad(a.reshape(1, -1), ((0, 0), (0, LANES - a.shape[-1])))
    head_of_lane = jnp.arange(inner, dtype=jnp.int32) // M2_HEAD_DIM
    sel = (jnp.arange(LANES, dtype=jnp.int32)[:, None] == head_of_lane[None, :]).astype(BF16)
    d_x = jnp.repeat(d_skip, M2_HEAD_DIM).reshape(1, inner)
    const = lambda shape: pl.BlockSpec(shape, lambda b, c: (0, 0))
    return pl.pallas_call(
        _m2_kernel, grid=(batch // nb, seq_len // CHUNK),
        in_specs=[pl.BlockSpec((nb, CHUNK, W), lambda b, c: (b, c, 0)),
                  _prev_rows_spec(W),
                  pl.BlockSpec((nb, CHUNK, LANES), lambda b, c: (b, c, 0)),
                  const((CONV_W, nconv)), const((1, nconv)), const((1, LANES)),
                  const((1, LANES)), const((1, inner)), const((1, inner)),
                  const((LANES, inner))],
        out_specs=pl.BlockSpec((nb, CHUNK, inner), lambda b, c: (b, c, 0)),
        out_shape=jax.ShapeDtypeStruct((batch, seq_len, inner), BF16),
        scratch_shapes=[pltpu.VMEM((nb, HALO + CHUNK, nconv), BF16),
                        pltpu.VMEM((nb, M2_GROUPS, M2_STATE, inner // M2_GROUPS), F32)],
        compiler_params=_cparams("parallel", "arbitrary"), name="m2_core",
    )(proj, proj, dt_raw, conv_w, conv_b.reshape(1, nconv), pad(dt_bias), pad(a_log),
      d_x, norm_g.reshape(1, inner), sel)


def m2_mixer(h, w_in_stack, layer,conv_w, conv_b, dt_bias, a_log, d_skip, norm_g,
             *, batch, seq_len):
    heads = dt_bias.shape[0]
    nmain = w_in_stack.shape[2] - heads
    w_small = jnp.pad(w_in_stack[layer][:, nmain:], ((0, 0), (0, LANES - heads)))
    proj, dt_raw = norm_matmul(h, w_in_stack, layer,nmain, w_small, batch=batch,
                               out_dtype=BF16)
    y = m2_core(proj, dt_raw, conv_w, conv_b, dt_bias, a_log, d_skip, norm_g)
    return y.reshape(batch * seq_len, -1)


def _s5_prep_kernel(lr_ref, li_ref, ldt_ref, br_ref, bi_ref, lbr_ref, lbi_ref, bbr_ref, bbi_ref):
    lr, li = lr_ref[...], li_ref[...]
    dt = jnp.exp(ldt_ref[...])
    mag = jnp.exp(lr * dt)
    lbr = mag * jnp.cos(li * dt)
    lbi = mag * jnp.sin(li * dt)
    nr = lbr - 1.0
    den = lr * lr + li * li
    qr = (nr * lr + lbi * li) / den
    qi = (lbi * lr - nr * li) / den
    br, bi = br_ref[...], bi_ref[...]
    lbr_ref[...] = lbr
    lbi_ref[...] = lbi
    bbr_ref[...] = qr * br - qi * bi
    bbi_ref[...] = qr * bi + qi * br


def _s5_kernel(u_ref, b_ref, cre_ref, cim_ref, lam_ref, d_ref, y_ref, bu_ref, st_ref, *, steps):
    nslab = b_ref.shape[0]
    half = b_ref.shape[2] // 2

    @pl.when(pl.program_id(0) == 0)
    def _():
        st_ref[...] = jnp.zeros_like(st_ref)

    batch, _, d_model = u_ref.shape
    rows = batch * steps

    dst = lax.broadcasted_iota(jnp.int32, (rows, rows), 0)
    src = lax.broadcasted_iota(jnp.int32, (rows, rows), 1)
    perm = jnp.where((dst // batch == src % steps) & (dst % batch == src // steps),
                     1.0, 0.0).astype(BF16)
    u_hi, u_lo = _split2(u_ref[...].reshape(rows, d_model))
    u_hi = _dot(perm, u_hi)
    u_tm = u_hi + _dot(perm, u_lo)
    ub = u_hi.astype(BF16)

    for s in range(nslab):
        bu_ref[:, 2 * half * s:2 * half * (s + 1)] = _dot(
            ub[:, s * S5_SLAB:(s + 1) * S5_SLAB], b_ref[s])

    for s in range(nslab):
        re = slice(2 * half * s, 2 * half * s + half)
        im = slice(2 * half * s + half, 2 * half * (s + 1))
        lam_r = jnp.broadcast_to(lam_ref[:, re], (batch, half))
        lam_i = jnp.broadcast_to(lam_ref[:, im], (batch, half))

        def step(t, carry, re=re, im=im, lam_r=lam_r, lam_i=lam_i):
            xr, xi = carry
            at_t = pl.ds(pl.multiple_of(t * batch, batch), batch)
            nr = lam_r * xr - lam_i * xi + bu_ref[at_t, re]
            ni = lam_r * xi + lam_i * xr + bu_ref[at_t, im]
            bu_ref[at_t, re] = nr
            bu_ref[at_t, im] = ni
            return nr, ni

        xr, xi = lax.fori_loop(0, steps, step, (st_ref[:, re], st_ref[:, im]))
        st_ref[:, re] = xr
        st_ref[:, im] = xi

    ys = []
    for s in range(nslab):
        re = slice(2 * half * s, 2 * half * s + half)
        im = slice(2 * half * s + half, 2 * half * (s + 1))
        cols = slice(s * S5_SLAB, (s + 1) * S5_SLAB)
        y = (_dot(bu_ref[:, re].astype(BF16), cre_ref[s])
             - _dot(bu_ref[:, im].astype(BF16), cim_ref[s])
             + d_ref[:, cols] * u_tm[:, cols])
        ys.append(jax.nn.gelu(y, approximate=True).astype(BF16))
    y_sm = _dot_tn(perm, jnp.concatenate(ys, axis=1))
    y_ref[...] = y_sm.astype(y_ref.dtype).reshape(batch, steps, d_model)


def s5_core(u, lam_re, lam_im, log_dt, b_re, b_im, c_re, c_im, d_skip, *, steps=32):
    batch, L, D = u.shape
    assert batch == SUBLANES, "one time step's batch entries fill one sublane tile"
    steps = min(steps, L)
    G, P = lam_re.shape
    K = S5_GROUP
    nslab = D // S5_SLAB
    gps = S5_SLAB // K
    half = gps * P

    rep = lambda a: jnp.repeat(a, K, axis=-1)
    flat = jax.ShapeDtypeStruct((G, P * K), F32)
    lbr, lbi, bbr, bbi = pl.pallas_call(
        _s5_prep_kernel, out_shape=[flat] * 4, name="s5_prep",
    )(rep(lam_re), rep(lam_im), jnp.broadcast_to(log_dt[:, None], (G, P * K)),
      b_re.reshape(G, P * K), b_im.reshape(G, P * K))

    eye = jnp.eye(gps, dtype=F32)
    to_slab = lambda a: a.reshape(G, P, K).transpose(0, 2, 1).reshape(nslab, gps, K, P)
    b_blk = jnp.concatenate(
        [jnp.einsum('sjkp,jJ->sjkJp', to_slab(a), eye).reshape(nslab, S5_SLAB, half)
         for a in (bbr, bbi)], axis=-1).astype(BF16)
    c_blk = [jnp.einsum('sjkp,jJ->sJpjk', a.reshape(nslab, gps, K, P), eye)
             .reshape(nslab, half, S5_SLAB).astype(BF16) for a in (c_re, c_im)]
    lam = jnp.stack([lbr[:, ::K].reshape(nslab, half), lbi[:, ::K].reshape(nslab, half)],
                    axis=1).reshape(1, nslab * 2 * half)

    const3 = lambda shape: pl.BlockSpec(shape, lambda i: (0, 0, 0))
    return pl.pallas_call(
        functools.partial(_s5_kernel, steps=steps), grid=(L // steps,),
        in_specs=[pl.BlockSpec((batch, steps, D), lambda i: (0, i, 0)),
                  const3((nslab, S5_SLAB, 2 * half)),
                  const3((nslab, half, S5_SLAB)), const3((nslab, half, S5_SLAB)),
                  pl.BlockSpec((1, nslab * 2 * half), lambda i: (0, 0)),
                  pl.BlockSpec((1, D), lambda i: (0, 0))],
        out_specs=pl.BlockSpec((batch, steps, D), lambda i: (0, i, 0)),
        out_shape=jax.ShapeDtypeStruct((batch, L, D), BF16),
        scratch_shapes=[pltpu.VMEM((batch * steps, nslab * 2 * half), F32),
                        pltpu.VMEM((batch, nslab * 2 * half), F32)],
        compiler_params=_cparams("arbitrary"), name="s5_core",
    )(u, b_blk, c_blk[0], c_blk[1], lam, d_skip.reshape(1, D))


def s5_mixer(h, w_in_stack, layer,lam_re, lam_im, log_dt, b_re, b_im, c_re, c_im, d_skip,
             *, batch, seq_len):
    T, D = h.shape
    u = norm_matmul(h, w_in_stack, layer,D, batch=batch, out_dtype=F32)
    y = s5_core(u, lam_re, lam_im, log_dt, b_re, b_im, c_re, c_im, d_skip)
    return y.reshape(T, D)


def kernel(x, norm_mix_g, norm_mlp_g, mlp_w1, mlp_w2, gdn_w_in, gdn_conv_w, gdn_a_log, gdn_dt_bias, gdn_o_norm_g, gdn_w_out, s5_w_in, s5_lam_re, s5_lam_im, s5_log_dt, s5_b_re, s5_b_im, s5_c_re, s5_c_im, s5_d, s5_w_out, m2_w_in, m2_conv_w, m2_conv_b, m2_dt_bias, m2_a_log, m2_d, m2_norm_g, m2_w_out, final_norm_g):
    batch, seq_len, d_model = x.shape
    depth = norm_mix_g.shape[0]
    h = x.reshape(batch * seq_len, d_model)
    mlp_w1, mlp_w2 = mlp_w1.astype(BF16), mlp_w2.astype(BF16)
    gdn_w_in, s5_w_in, m2_w_in = (
        (norm_mix_g[kind::3][:, :, None] * w).astype(BF16)
        for kind, w in enumerate((gdn_w_in, s5_w_in, m2_w_in)))
    for i in range(depth):
        kind, j = i % 3, i // 3
        if kind == 0:
            mix = gdn_mixer(h,gdn_w_in, j, gdn_conv_w[j], gdn_a_log[j],
                            gdn_dt_bias[j], gdn_o_norm_g[j], batch=batch, seq_len=seq_len)
            w_out = gdn_w_out[j]
        elif kind == 1:
            mix = s5_mixer(h,s5_w_in, j, s5_lam_re[j], s5_lam_im[j], s5_log_dt[j],
                           s5_b_re[j], s5_b_im[j], s5_c_re[j], s5_c_im[j], s5_d[j],
                           batch=batch, seq_len=seq_len)
            w_out = s5_w_out[j]
        else:
            mix = m2_mixer(h,m2_w_in, j, m2_conv_w[j], m2_conv_b[j], m2_dt_bias[j],
                           m2_a_log[j], m2_d[j], m2_norm_g[j], batch=batch, seq_len=seq_len)
            w_out = m2_w_out[j]
        h = out_proj_mlp(mix, w_out.astype(BF16), h, norm_mlp_g[i], mlp_w1, mlp_w2, i,
                         final_norm_g if i == depth - 1 else None, glu=(kind == 1))
    return h.reshape(batch, seq_len, d_model)
```
